```python
import math
import jax, jax.numpy as jnp
from jax import lax
import numpy as np

D_MODEL = 1024
BATCH = 4
SEQ = 4096
DEPTH = 2

MLA_HEADS = 8
MLA_Q_LORA = 256
MLA_KV_LORA = 128
MLA_NOPE = 64
MLA_ROPE = 32
MLA_V = 64
MLA_WIDTH = MLA_HEADS * MLA_V
DIFF_HEADS = 4
DIFF_HEAD_DIM = 64
DIFF_WIDTH = DIFF_HEADS * 2 * DIFF_HEAD_DIM
DIFF_ROT = DIFF_HEAD_DIM // 4
ROPE_THETA = 500000.0
Q_BLOCK = 128
DEEPNORM_ALPHA = (2 * DEPTH) ** 0.25
DEEPNORM_BETA = (8 * DEPTH) ** -0.25
LN_EPS = 1e-5
RMS_EPS = 1e-6
IN_SPLITS = (MLA_Q_LORA, MLA_KV_LORA, MLA_ROPE, MLA_WIDTH,
             DIFF_WIDTH, DIFF_WIDTH, DIFF_WIDTH, DIFF_WIDTH, 2 * D_MODEL)
IN_COLS = sum(IN_SPLITS)

kernel_name = "hybrid_mla_diffattn_gated_deepnorm"


def _rms_norm(x, g, eps=RMS_EPS):
    xf = x.astype(jnp.float32)
    y = xf * lax.rsqrt(jnp.mean(xf * xf, axis=-1, keepdims=True) + eps)
    return (y * g.astype(jnp.float32)).astype(x.dtype)


def _layer_norm(x, g, b, eps=LN_EPS):
    xf = x.astype(jnp.float32)
    mu = jnp.mean(xf, axis=-1, keepdims=True)
    var = jnp.mean(jnp.square(xf - mu), axis=-1, keepdims=True)
    y = (xf - mu) * lax.rsqrt(var + eps)
    return (y * g.astype(jnp.float32) + b.astype(jnp.float32)).astype(x.dtype)


def _rotary(x, rot_dim):
    seq = x.shape[1]
    half = rot_dim // 2
    inv_freq = ROPE_THETA ** (-jnp.arange(half, dtype=jnp.float32) / half)
    ang = jnp.arange(seq, dtype=jnp.float32)[:, None] * inv_freq[None, :]
    shape = (seq,) + (1,) * (x.ndim - 3) + (half,)
    cos = jnp.cos(ang).reshape(shape)
    sin = jnp.sin(ang).reshape(shape)
    xr = x[..., :rot_dim].astype(jnp.float32)
    x1, x2 = xr[..., :half], xr[..., half:]
    rot = jnp.concatenate([x1 * cos - x2 * sin, x2 * cos + x1 * sin], axis=-1).astype(x.dtype)
    return jnp.concatenate([rot, x[..., rot_dim:]], axis=-1)


def _split_cols(h):
    parts, start = [], 0
    for size in IN_SPLITS:
        parts.append(h[..., start:start + size])
        start += size
    return parts


def _dense_attention(q, k, v, scale):
    b, h, s, dk = q.shape
    nb = s // Q_BLOCK
    qb = q.reshape(b, h, nb, Q_BLOCK, dk).transpose(2, 0, 1, 3, 4)

    def one_block(q_blk):
        sc = jnp.einsum('bhqd,bhkd->bhqk', q_blk, k, preferred_element_type=jnp.float32) * scale
        p = jax.nn.softmax(sc, axis=-1).astype(v.dtype)
        return jnp.einsum('bhqk,bhkd->bhqd', p, v)

    o = lax.map(one_block, qb)
    return o.transpose(1, 2, 0, 3, 4).reshape(b, h, s, v.shape[-1])


def _differential_attention(q, k, v, lam, scale):
    b, h, two, s, d = q.shape
    nb = s // Q_BLOCK
    qb = q.reshape(b, h, two, nb, Q_BLOCK, d).transpose(3, 0, 1, 2, 4, 5)

    def one_block(q_blk):
        sc = jnp.einsum('bhmqd,bhmkd->bhmqk', q_blk, k, preferred_element_type=jnp.float32) * scale
        p = jax.nn.softmax(sc, axis=-1)
        p_diff = (p[:, :, 0] - lam * p[:, :, 1]).astype(v.dtype)
        return jnp.einsum('bhqk,bhkd->bhqd', p_diff, v)

    o = lax.map(one_block, qb)
    return o.transpose(1, 2, 0, 3, 4).reshape(b, h, s, v.shape[-1])


def setup_inputs(seed: int = 0) -> dict:
    key = jax.random.key(seed)
    ks = jax.random.split(key, 16)
    f32 = jnp.float32

    def nrm(k, shape, scale):
        return jax.random.normal(k, shape, f32) * scale

    return {
        "x": jax.random.normal(ks[0], (BATCH, SEQ, D_MODEL), f32),
        "w_in": nrm(ks[1], (DEPTH, D_MODEL, IN_COLS), D_MODEL ** -0.5),
        "g_q": 1.0 + nrm(ks[2], (DEPTH, MLA_Q_LORA), 0.02),
        "w_q_up": nrm(ks[3], (DEPTH, MLA_Q_LORA, MLA_HEADS * (MLA_NOPE + MLA_ROPE)), MLA_Q_LORA ** -0.5),
        "g_kv": 1.0 + nrm(ks[4], (DEPTH, MLA_KV_LORA), 0.02),
        "w_kv_up": nrm(ks[5], (DEPTH, MLA_KV_LORA, MLA_HEADS * (MLA_NOPE + MLA_V)), MLA_KV_LORA ** -0.5),
        "diff_lambda": nrm(ks[6], (DEPTH, 4, DIFF_HEAD_DIM), 0.1),
        "g_diff": 1.0 + nrm(ks[7], (DEPTH, 2 * DIFF_HEAD_DIM), 0.02),
        "w_branch_a": nrm(ks[8], (DEPTH, MLA_WIDTH, D_MODEL), MLA_WIDTH ** -0.5),
        "w_branch_b": nrm(ks[9], (DEPTH, DIFF_WIDTH, D_MODEL), DIFF_WIDTH ** -0.5),
        "b_merge": nrm(ks[10], (DEPTH, 2 * D_MODEL), 0.02),
        "w_out": nrm(ks[11], (DEPTH, D_MODEL, D_MODEL), D_MODEL ** -0.5 * DEEPNORM_BETA),
        "ln_gamma": 1.0 + nrm(ks[12], (DEPTH, D_MODEL), 0.02),
        "ln_beta": nrm(ks[13], (DEPTH, D_MODEL), 0.02),
    }


def reference(x, w_in, g_q, w_q_up, g_kv, w_kv_up, diff_lambda, g_diff,
              w_branch_a, w_branch_b, b_merge, w_out, ln_gamma, ln_beta):
    b, s, _ = x.shape
    mla_scale = 1.0 / math.sqrt(MLA_NOPE + MLA_ROPE)
    diff_scale = 1.0 / math.sqrt(DIFF_HEAD_DIM)
    for l in range(DEPTH):
        h = x @ w_in[l]
        c_q, c_kv, k_r, gate_a, q_d, k_d, v_d, gate_b, gates = _split_cols(h)

        q_a = (_rms_norm(c_q, g_q[l]) @ w_q_up[l]).reshape(b, s, MLA_HEADS, MLA_NOPE + MLA_ROPE)
        q_a = jnp.concatenate([q_a[..., :MLA_NOPE], _rotary(q_a[..., MLA_NOPE:], MLA_ROPE)], axis=-1)
        kv = (_rms_norm(c_kv, g_kv[l]) @ w_kv_up[l]).reshape(b, s, MLA_HEADS, MLA_NOPE + MLA_V)
        k_rope = _rotary(k_r, MLA_ROPE)
        k_a = jnp.concatenate(
            [kv[..., :MLA_NOPE], jnp.broadcast_to(k_rope[:, :, None, :], (b, s, MLA_HEADS, MLA_ROPE))], axis=-1)
        v_a = kv[..., MLA_NOPE:]
        o_a = _dense_attention(q_a.transpose(0, 2, 1, 3), k_a.transpose(0, 2, 1, 3),
                               v_a.transpose(0, 2, 1, 3), mla_scale)
        y_a = o_a.transpose(0, 2, 1, 3).reshape(b, s, MLA_WIDTH) * jax.nn.silu(gate_a)
        y_a = y_a @ w_branch_a[l]

        q_b = _rotary(q_d.reshape(b, s, DIFF_HEADS, 2, DIFF_HEAD_DIM), DIFF_ROT)
        k_b = _rotary(k_d.reshape(b, s, DIFF_HEADS, 2, DIFF_HEAD_DIM), DIFF_ROT)
        v_b = v_d.reshape(b, s, DIFF_HEADS, 2 * DIFF_HEAD_DIM)
        lam_init = 0.8 - 0.6 * math.exp(-0.3 * l)
        lp = diff_lambda[l].astype(jnp.float32)
        lam = jnp.exp(jnp.sum(lp[0] * lp[1])) - jnp.exp(jnp.sum(lp[2] * lp[3])) + lam_init
        o_b = _differential_attention(q_b.transpose(0, 2, 3, 1, 4), k_b.transpose(0, 2, 3, 1, 4),
                                      v_b.transpose(0, 2, 1, 3), lam, diff_scale)
        o_b = _rms_norm(o_b, g_diff[l], eps=1e-5) * (1.0 - lam_init)
        y_b = o_b.transpose(0, 2, 1, 3).reshape(b, s, DIFF_WIDTH) * jax.nn.silu(gate_b)
        y_b = y_b @ w_branch_b[l]

        g = jax.nn.sigmoid(gates + b_merge[l])
        merged = g[..., :D_MODEL] * y_a + g[..., D_MODEL:] * y_b
        out = merged @ w_out[l]
        x = _layer_norm(DEEPNORM_ALPHA * x + out, ln_gamma[l], ln_beta[l])
    return x
```

```python
import functools
import math

import jax
import jax.numpy as jnp
from jax import lax
from jax.experimental import pallas as pl
from jax.experimental.pallas import tpu as pltpu

D_MODEL = 1024
DEPTH = 2
MLA_HEADS = 8
MLA_Q_LORA = 256
MLA_KV_LORA = 128
MLA_NOPE = 64
MLA_ROPE = 32
MLA_V = 64
MLA_WIDTH = MLA_HEADS * MLA_V
DIFF_HEADS = 4
DIFF_HEAD_DIM = 64
DIFF_WIDTH = DIFF_HEADS * 2 * DIFF_HEAD_DIM
DIFF_ROT = DIFF_HEAD_DIM // 4
ROPE_THETA = 500000.0
DEEPNORM_ALPHA = (2 * DEPTH) ** 0.25
LN_EPS = 1e-5
RMS_EPS = 1e-6
DIFF_RMS_EPS = 1e-5

LANES = 128
TOKEN_TILE = 256
KEY_CHUNK = 512
VMEM_LIMIT_BYTES = 56 * 1024 * 1024

_C_Q = 0
_C_KV = _C_Q + MLA_Q_LORA
_C_KR = _C_KV + MLA_KV_LORA
_C_GA = _C_KR + LANES
_C_QD = _C_GA + MLA_WIDTH
_C_KD = _C_QD + DIFF_WIDTH
_C_VD = _C_KD + DIFF_WIDTH
_C_GB = _C_VD + DIFF_WIDTH
_C_GM = _C_GB + DIFF_WIDTH
_C_END = _C_GM + 2 * D_MODEL


def _rotary_tables(seq, rot_dim, first_lane, period):
    half = rot_dim // 2
    inv_freq = ROPE_THETA ** (-jnp.arange(half, dtype=jnp.float32) / half)
    ang = jnp.arange(seq, dtype=jnp.float32)[:, None] * inv_freq[None, :]
    cos, sin = jnp.cos(ang), jnp.sin(ang)
    lane = jnp.arange(LANES) % period - first_lane
    lo = (lane >= 0) & (lane < half)
    hi = (lane >= half) & (lane < rot_dim)
    idx = jnp.clip(jnp.where(hi, lane - half, lane), 0, half - 1)
    cos_l, sin_l = cos[:, idx], sin[:, idx]
    c = jnp.where((lo | hi)[None, :], cos_l, 1.0)
    s_lo = jnp.where(lo[None, :], -sin_l, 0.0)
    s_hi = jnp.where(hi[None, :], sin_l, 0.0)
    return jnp.stack([c, s_lo, s_hi]).astype(jnp.float32)


def _rotate(x, tab_ref, half):
    up = pltpu.roll(x, LANES - half, 1)
    down = pltpu.roll(x, half, 1)
    return x * tab_ref[0] + up * tab_ref[1] + down * tab_ref[2]


def _rms_norm(x, g, eps):
    return x * lax.rsqrt(jnp.mean(x * x, axis=-1, keepdims=True) + eps) * g


def _proj_kernel(x_ref, w1_ref, wq_ref, wkn_ref, wv_ref, gq_ref, gkv_ref, bm_ref,
                 taba_ref, tabb_ref,
                 ka_ref, qat_ref, vat_ref, gat_ref, kb_ref, qbt_ref, vbt_ref, gbt_ref, gm_ref):
    bf16, f32 = jnp.bfloat16, jnp.float32
    xb = x_ref[0].astype(bf16)

    def proj(lo, hi):
        return jnp.dot(xb, w1_ref[:, lo:hi], preferred_element_type=f32)

    cq = _rms_norm(proj(_C_Q, _C_KV), gq_ref[...], RMS_EPS).astype(bf16)
    ckv = _rms_norm(proj(_C_KV, _C_KR), gkv_ref[...], RMS_EPS).astype(bf16)
    k_rope = _rotate(proj(_C_KR, _C_GA), taba_ref, MLA_ROPE // 2)
    mla_scale = 1.0 / math.sqrt(MLA_NOPE + MLA_ROPE)
    for h in range(MLA_HEADS):
        sl = slice(h * LANES, (h + 1) * LANES)
        q_h = jnp.dot(cq, wq_ref[:, sl], preferred_element_type=f32)
        q_h = _rotate(q_h, taba_ref, MLA_ROPE // 2) * mla_scale
        qat_ref[0, h, 0] = q_h.T.astype(bf16)
        k_h = jnp.dot(ckv, wkn_ref[:, sl], preferred_element_type=f32) + k_rope
        ka_ref[0, :, sl] = k_h.astype(bf16)
    v = jnp.dot(ckv, wv_ref[...], preferred_element_type=f32)
    vat_ref[0] = v.T.reshape(MLA_HEADS, MLA_V, -1).astype(bf16)
    ga = proj(_C_GA, _C_QD)
    gat_ref[0] = (ga * jax.nn.sigmoid(ga)).T.reshape(MLA_HEADS, MLA_V, -1)[:, None].astype(bf16)

    diff_scale = 1.0 / math.sqrt(DIFF_HEAD_DIM)
    for h in range(DIFF_HEADS):
        sl = slice(h * LANES, (h + 1) * LANES)
        q_h = _rotate(proj(_C_QD + h * LANES, _C_QD + (h + 1) * LANES), tabb_ref, DIFF_ROT // 2)
        qbt_ref[0, h, 0] = (q_h * diff_scale).T.astype(bf16)
        k_h = _rotate(proj(_C_KD + h * LANES, _C_KD + (h + 1) * LANES), tabb_ref, DIFF_ROT // 2)
        kb_ref[0, :, sl] = k_h.astype(bf16)
    vbt_ref[0] = proj(_C_VD, _C_GB).T.reshape(DIFF_HEADS, LANES, -1).astype(bf16)
    gb = proj(_C_GB, _C_GM)
    gbt_ref[0] = (gb * jax.nn.sigmoid(gb)).T.reshape(DIFF_HEADS, LANES, -1)[:, None].astype(bf16)

    gm_ref[0] = jax.nn.sigmoid(proj(_C_GM, _C_END) + bm_ref[...]).astype(bf16)


def _softmax_step(s, m, l):
    m_new = jnp.maximum(m, jnp.max(s, axis=0, keepdims=True))
    p = jnp.exp(s - m_new)
    alpha = jnp.exp(m - m_new)
    l_new = alpha * l + jnp.sum(p, axis=0, keepdims=True)
    return p.astype(jnp.bfloat16), alpha, m_new, l_new


def _mla_attn_kernel(k_ref, qt_ref, vt_ref, gt_ref, o_ref, *, seq, n_q):
    f32 = jnp.float32
    tq = qt_ref.shape[-1]

    def q_tile(qi, carry):
        qt = qt_ref[0, 0, qi]
        m = jnp.full((1, tq), -jnp.inf, f32)
        l = jnp.zeros((1, tq), f32)
        acc = jnp.zeros((MLA_V, tq), f32)
        for c in range(seq // KEY_CHUNK):
            ks = slice(c * KEY_CHUNK, (c + 1) * KEY_CHUNK)
            s = jnp.dot(k_ref[0, ks, :], qt, preferred_element_type=f32)
            p, alpha, m, l = _softmax_step(s, m, l)
            acc = alpha * acc + jnp.dot(vt_ref[0, 0, :, ks], p, preferred_element_type=f32)
        o = acc * (1.0 / l)
        o_ref[0, 0, qi] = (o * gt_ref[0, 0, qi].astype(f32)).astype(o_ref.dtype)
        return carry

    lax.fori_loop(0, n_q, q_tile, 0)


def _diff_attn_kernel(k_ref, qt_ref, vt_ref, gt_ref, lam_ref, gd_ref, o_ref, *, seq, n_q, lam_init):
    f32 = jnp.float32
    tq = qt_ref.shape[-1]
    lp = lam_ref[...].astype(f32)
    lam = (jnp.exp(jnp.sum(lp[0:1] * lp[1:2], axis=1, keepdims=True))
           - jnp.exp(jnp.sum(lp[2:3] * lp[3:4], axis=1, keepdims=True)) + lam_init)
    first = lax.broadcasted_iota(jnp.int32, (LANES, tq), 0) < DIFF_HEAD_DIM

    def q_tile(qi, carry):
        qt = qt_ref[0, 0, qi]
        zero = jnp.zeros_like(qt)
        q_maps = (jnp.where(first, qt, zero), jnp.where(first, zero, qt))
        m = [jnp.full((1, tq), -jnp.inf, f32) for _ in range(2)]
        l = [jnp.zeros((1, tq), f32) for _ in range(2)]
        acc = [jnp.zeros((LANES, tq), f32) for _ in range(2)]
        for c in range(seq // KEY_CHUNK):
            ks = slice(c * KEY_CHUNK, (c + 1) * KEY_CHUNK)
            k = k_ref[0, ks, :]
            vt = vt_ref[0, 0, :, ks]
            for i in range(2):
                s = jnp.dot(k, q_maps[i], preferred_element_type=f32)
                p, alpha, m[i], l[i] = _softmax_step(s, m[i], l[i])
                acc[i] = alpha * acc[i] + jnp.dot(vt, p, preferred_element_type=f32)
        o = acc[0] * (1.0 / l[0]) - lam * (acc[1] * (1.0 / l[1]))
        o = o * lax.rsqrt(jnp.mean(o * o, axis=0, keepdims=True) + DIFF_RMS_EPS)
        o = o * gd_ref[...] * (1.0 - lam_init)
        o_ref[0, 0, qi] = (o * gt_ref[0, 0, qi].astype(f32)).astype(o_ref.dtype)
        return carry

    lax.fori_loop(0, n_q, q_tile, 0)


def _out_kernel(x_ref, za_ref, zb_ref, gm_ref, wa_ref, wb_ref, wo_ref, lg_ref, lb_ref, o_ref):
    bf16, f32 = jnp.bfloat16, jnp.float32
    tm = x_ref.shape[1]
    za = za_ref[0].astype(f32).reshape(MLA_WIDTH, tm).T.astype(bf16)
    zb = zb_ref[0].astype(f32).reshape(DIFF_WIDTH, tm).T.astype(bf16)
    ya = jnp.dot(za, wa_ref[...], preferred_element_type=f32)
    yb = jnp.dot(zb, wb_ref[...], preferred_element_type=f32)
    gm = gm_ref[0].astype(f32)
    merged = gm[:, :D_MODEL] * ya + gm[:, D_MODEL:] * yb
    out = jnp.dot(merged.astype(bf16), wo_ref[...], preferred_element_type=f32)
    r = DEEPNORM_ALPHA * x_ref[0] + out
    mu = jnp.mean(r, axis=-1, keepdims=True)
    d = r - mu
    var = jnp.mean(d * d, axis=-1, keepdims=True)
    o_ref[0] = d * lax.rsqrt(var + LN_EPS) * lg_ref[...] + lb_ref[...]


def _const_spec(shape):
    return pl.BlockSpec(shape, lambda *_: (0,) * len(shape))


def _layer(x, p, l, tab_a, tab_b):
    bf16, f32 = jnp.bfloat16, jnp.float32
    batch, seq, _ = x.shape
    tm = TOKEN_TILE
    n_q = seq // tm
    cparams = functools.partial(pltpu.CompilerParams, vmem_limit_bytes=VMEM_LIMIT_BYTES)

    w_in = p["w_in"][l]
    offs = [0]
    for width in (MLA_Q_LORA, MLA_KV_LORA, MLA_ROPE, MLA_WIDTH, DIFF_WIDTH, DIFF_WIDTH,
                  DIFF_WIDTH, DIFF_WIDTH, 2 * D_MODEL):
        offs.append(offs[-1] + width)
    kr_cols = jnp.pad(w_in[:, offs[2]:offs[3]], ((0, 0), (MLA_NOPE, LANES - MLA_NOPE - MLA_ROPE)))
    w1 = jnp.concatenate([w_in[:, :offs[2]], kr_cols, w_in[:, offs[3]:]], axis=1).astype(bf16)
    wq = p["w_q_up"][l].reshape(MLA_Q_LORA, MLA_HEADS, MLA_NOPE + MLA_ROPE)
    wq = jnp.pad(wq, ((0, 0), (0, 0), (0, LANES - MLA_NOPE - MLA_ROPE)))
    wq = wq.reshape(MLA_Q_LORA, MLA_HEADS * LANES).astype(bf16)
    wkv = p["w_kv_up"][l].reshape(MLA_KV_LORA, MLA_HEADS, MLA_NOPE + MLA_V)
    wkn = jnp.pad(wkv[:, :, :MLA_NOPE], ((0, 0), (0, 0), (0, LANES - MLA_NOPE)))
    wkn = wkn.reshape(MLA_KV_LORA, MLA_HEADS * LANES).astype(bf16)
    wv = wkv[:, :, MLA_NOPE:].reshape(MLA_KV_LORA, MLA_WIDTH).astype(bf16)

    tok = lambda b, i: (b, i, 0)
    tile5 = lambda b, i: (b, 0, i, 0, 0)
    tab_spec = pl.BlockSpec((3, tm, LANES), lambda b, i: (0, i, 0))
    ka, qat, vat, gat, kb, qbt, vbt, gbt, gm = pl.pallas_call(
        _proj_kernel,
        grid=(batch, n_q),
        in_specs=[
            pl.BlockSpec((1, tm, D_MODEL), tok),
            _const_spec(w1.shape), _const_spec(wq.shape), _const_spec(wkn.shape), _const_spec(wv.shape),
            _const_spec((1, MLA_Q_LORA)), _const_spec((1, MLA_KV_LORA)), _const_spec((1, 2 * D_MODEL)),
            tab_spec, tab_spec,
        ],
        out_specs=[
            pl.BlockSpec((1, tm, MLA_HEADS * LANES), tok),
            pl.BlockSpec((1, MLA_HEADS, 1, LANES, tm), tile5),
            pl.BlockSpec((1, MLA_HEADS, MLA_V, tm), lambda b, i: (b, 0, 0, i)),
            pl.BlockSpec((1, MLA_HEADS, 1, MLA_V, tm), tile5),
            pl.BlockSpec((1, tm, DIFF_WIDTH), tok),
            pl.BlockSpec((1, DIFF_HEADS, 1, LANES, tm), tile5),
            pl.BlockSpec((1, DIFF_HEADS, LANES, tm), lambda b, i: (b, 0, 0, i)),
            pl.BlockSpec((1, DIFF_HEADS, 1, LANES, tm), tile5),
            pl.BlockSpec((1, tm, 2 * D_MODEL), tok),
        ],
        out_shape=[
            jax.ShapeDtypeStruct((batch, seq, MLA_HEADS * LANES), bf16),
            jax.ShapeDtypeStruct((batch, MLA_HEADS, n_q, LANES, tm), bf16),
            jax.ShapeDtypeStruct((batch, MLA_HEADS, MLA_V, seq), bf16),
            jax.ShapeDtypeStruct((batch, MLA_HEADS, n_q, MLA_V, tm), bf16),
            jax.ShapeDtypeStruct((batch, seq, DIFF_WIDTH), bf16),
            jax.ShapeDtypeStruct((batch, DIFF_HEADS, n_q, LANES, tm), bf16),
            jax.ShapeDtypeStruct((batch, DIFF_HEADS, LANES, seq), bf16),
            jax.ShapeDtypeStruct((batch, DIFF_HEADS, n_q, LANES, tm), bf16),
            jax.ShapeDtypeStruct((batch, seq, 2 * D_MODEL), bf16),
        ],
        compiler_params=cparams(dimension_semantics=("arbitrary", "arbitrary")),
        name="proj",
    )(x, w1, wq, wkn, wv, p["g_q"][l][None], p["g_kv"][l][None], p["b_merge"][l][None], tab_a, tab_b)

    head4 = lambda b, h: (b, h, 0, 0)
    head5 = lambda b, h: (b, h, 0, 0, 0)
    za = pl.pallas_call(
        functools.partial(_mla_attn_kernel, seq=seq, n_q=n_q),
        grid=(batch, MLA_HEADS),
        in_specs=[
            pl.BlockSpec((1, seq, LANES), lambda b, h: (b, 0, h)),
            pl.BlockSpec((1, 1, n_q, LANES, tm), head5),
            pl.BlockSpec((1, 1, MLA_V, seq), head4),
            pl.BlockSpec((1, 1, n_q, MLA_V, tm), head5),
        ],
        out_specs=pl.BlockSpec((1, 1, n_q, MLA_V, tm), head5),
        out_shape=jax.ShapeDtypeStruct((batch, MLA_HEADS, n_q, MLA_V, tm), bf16),
        compiler_params=cparams(dimension_semantics=("arbitrary", "arbitrary")),
        name="mla_attn",
    )(ka, qat, vat, gat)

    lam_init = 0.8 - 0.6 * math.exp(-0.3 * l)
    zb = pl.pallas_call(
        functools.partial(_diff_attn_kernel, seq=seq, n_q=n_q, lam_init=lam_init),
        grid=(batch, DIFF_HEADS),
        in_specs=[
            pl.BlockSpec((1, seq, LANES), lambda b, h: (b, 0, h)),
            pl.BlockSpec((1, 1, n_q, LANES, tm), head5),
            pl.BlockSpec((1, 1, LANES, seq), head4),
            pl.BlockSpec((1, 1, n_q, LANES, tm), head5),
            _const_spec((4, DIFF_HEAD_DIM)),
            _const_spec((2 * DIFF_HEAD_DIM, 1)),
        ],
        out_specs=pl.BlockSpec((1, 1, n_q, LANES, tm), head5),
        out_shape=jax.ShapeDtypeStruct((batch, DIFF_HEADS, n_q, LANES, tm), bf16),
        compiler_params=cparams(dimension_semantics=("arbitrary", "arbitrary")),
        name="diff_attn",
    )(kb, qbt, vbt, gbt, p["diff_lambda"][l], p["g_diff"][l][:, None])

    return pl.pallas_call(
        _out_kernel,
        grid=(batch, n_q),
        in_specs=[
            pl.BlockSpec((1, tm, D_MODEL), tok),
            pl.BlockSpec((1, MLA_HEADS, 1, MLA_V, tm), tile5),
            pl.BlockSpec((1, DIFF_HEADS, 1, LANES, tm), tile5),
            pl.BlockSpec((1, tm, 2 * D_MODEL), tok),
            _const_spec((MLA_WIDTH, D_MODEL)), _const_spec((DIFF_WIDTH, D_MODEL)),
            _const_spec((D_MODEL, D_MODEL)),
            _const_spec((1, D_MODEL)), _const_spec((1, D_MODEL)),
        ],
        out_specs=pl.BlockSpec((1, tm, D_MODEL), tok),
        out_shape=jax.ShapeDtypeStruct((batch, seq, D_MODEL), f32),
        compiler_params=cparams(dimension_semantics=("arbitrary", "arbitrary")),
        name="out_proj",
    )(x, za, zb, gm, p["w_branch_a"][l].astype(bf16), p["w_branch_b"][l].astype(bf16),
      p["w_out"][l].astype(bf16), p["ln_gamma"][l][None], p["ln_beta"][l][None])


def kernel(x, w_in, g_q, w_q_up, g_kv, w_kv_up, diff_lambda, g_diff, w_branch_a, w_branch_b,
           b_merge, w_out, ln_gamma, ln_beta):
    params = dict(w_in=w_in, g_q=g_q, w_q_up=w_q_up, g_kv=g_kv, w_kv_up=w_kv_up,
                  diff_lambda=diff_lambda, g_diff=g_diff, w_branch_a=w_branch_a,
                  w_branch_b=w_branch_b, b_merge=b_merge, w_out=w_out,
                  ln_gamma=ln_gamma, ln_beta=ln_beta)
    seq = x.shape[1]
    tab_a = _rotary_tables(seq, MLA_ROPE, MLA_NOPE, LANES)
    tab_b = _rotary_tables(seq, DIFF_ROT, 0, DIFF_HEAD_DIM)
    for l in range(DEPTH):
        x = _layer(x, params, l, tab_a, tab_b)
    return x
```

```python
import functools
import math

import jax
import jax.numpy as jnp
from jax import lax
from jax.experimental import pallas as pl
from jax.experimental.pallas import tpu as pltpu

D_MODEL = 1024
DEPTH = 2
MLA_HEADS = 8
MLA_Q_LORA = 256
MLA_KV_LORA = 128
MLA_NOPE = 64
MLA_ROPE = 32
MLA_V = 64
MLA_WIDTH = MLA_HEADS * MLA_V
DIFF_HEADS = 4
DIFF_HEAD_DIM = 64
DIFF_WIDTH = DIFF_HEADS * 2 * DIFF_HEAD_DIM
DIFF_ROT = DIFF_HEAD_DIM // 4
ROPE_THETA = 500000.0
DEEPNORM_ALPHA = (2 * DEPTH) ** 0.25
LN_EPS = 1e-5
RMS_EPS = 1e-6
DIFF_RMS_EPS = 1e-5
LOG2_E = math.log2(math.e)

LANES = 128
BF16_SUBLANES = 16
TOKEN_TILE = 256
KEY_CHUNK = 512
KEY_TILE = 256
ONES_ROWS = BF16_SUBLANES
VMEM_LIMIT_BYTES = 56 * 1024 * 1024

_C_Q = 0
_C_KV = _C_Q + MLA_Q_LORA
_C_KR = _C_KV + MLA_KV_LORA
_C_GA = _C_KR + LANES
_C_QD = _C_GA + MLA_WIDTH
_C_KD = _C_QD + DIFF_WIDTH
_C_VD = _C_KD + DIFF_WIDTH
_C_GB = _C_VD + DIFF_WIDTH
_C_GM = _C_GB + DIFF_WIDTH
_C_END = _C_GM + 2 * D_MODEL


def _rotary_tables(seq, rot_dim, first_lane, period):
    half = rot_dim // 2
    inv_freq = ROPE_THETA ** (-jnp.arange(half, dtype=jnp.float32) / half)
    ang = jnp.arange(seq, dtype=jnp.float32)[:, None] * inv_freq[None, :]
    cos, sin = jnp.cos(ang), jnp.sin(ang)
    lane = jnp.arange(LANES) % period - first_lane
    lo = (lane >= 0) & (lane < half)
    hi = (lane >= half) & (lane < rot_dim)
    idx = jnp.clip(jnp.where(hi, lane - half, lane), 0, half - 1)
    cos_l, sin_l = cos[:, idx], sin[:, idx]
    c = jnp.where((lo | hi)[None, :], cos_l, 1.0)
    s_lo = jnp.where(lo[None, :], -sin_l, 0.0)
    s_hi = jnp.where(hi[None, :], sin_l, 0.0)
    return jnp.stack([c, s_lo, s_hi]).astype(jnp.float32)


def _rotate(x, tab_ref, half):
    up = pltpu.roll(x, LANES - half, 1)
    down = pltpu.roll(x, half, 1)
    return x * tab_ref[0] + up * tab_ref[1] + down * tab_ref[2]


def _rms_norm(x, g, eps):
    return x * lax.rsqrt(jnp.mean(x * x, axis=-1, keepdims=True) + eps) * g


def _with_ones_rows(vt):
    heads, _, tokens = vt.shape
    return jnp.concatenate([vt, jnp.ones((heads, ONES_ROWS, tokens), vt.dtype)], axis=1)


def _proj_kernel(x_ref, w1_ref, wq_ref, wkn_ref, wv_ref, gq_ref, gkv_ref, bm_ref,
                 taba_ref, tabb_ref,
                 ka_ref, qat_ref, vat_ref, gat_ref, kb_ref, qbt_ref, vbt_ref, gbt_ref, gm_ref):
    bf16, f32 = jnp.bfloat16, jnp.float32
    xb = x_ref[0].astype(bf16)

    def proj(lo, hi):
        return jnp.dot(xb, w1_ref[:, lo:hi], preferred_element_type=f32)

    cq = _rms_norm(proj(_C_Q, _C_KV), gq_ref[...], RMS_EPS).astype(bf16)
    ckv = _rms_norm(proj(_C_KV, _C_KR), gkv_ref[...], RMS_EPS).astype(bf16)
    k_rope = _rotate(proj(_C_KR, _C_GA), taba_ref, MLA_ROPE // 2)
    mla_scale = LOG2_E / math.sqrt(MLA_NOPE + MLA_ROPE)
    for h in range(MLA_HEADS):
        sl = slice(h * LANES, (h + 1) * LANES)
        q_h = jnp.dot(cq, wq_ref[:, sl], preferred_element_type=f32)
        q_h = _rotate(q_h, taba_ref, MLA_ROPE // 2) * mla_scale
        qat_ref[0, h, 0] = q_h.T.astype(bf16)
        k_h = jnp.dot(ckv, wkn_ref[:, sl], preferred_element_type=f32) + k_rope
        ka_ref[0, :, sl] = k_h.astype(bf16)
    v = jnp.dot(ckv, wv_ref[...], preferred_element_type=f32)
    vat_ref[0] = _with_ones_rows(v.T.reshape(MLA_HEADS, MLA_V, -1)).astype(bf16)
    ga = proj(_C_GA, _C_QD)
    gat_ref[0] = (ga * jax.nn.sigmoid(ga)).T.reshape(MLA_HEADS, MLA_V, -1)[:, None].astype(bf16)

    diff_scale = LOG2_E / math.sqrt(DIFF_HEAD_DIM)
    for h in range(DIFF_HEADS):
        sl = slice(h * LANES, (h + 1) * LANES)
        q_h = _rotate(proj(_C_QD + h * LANES, _C_QD + (h + 1) * LANES), tabb_ref, DIFF_ROT // 2)
        qbt_ref[0, h, 0] = (q_h * diff_scale).T.astype(bf16)
        k_h = _rotate(proj(_C_KD + h * LANES, _C_KD + (h + 1) * LANES), tabb_ref, DIFF_ROT // 2)
        kb_ref[0, :, sl] = k_h.astype(bf16)
    vbt_ref[0] = _with_ones_rows(proj(_C_VD, _C_GB).T.reshape(DIFF_HEADS, LANES, -1)).astype(bf16)
    gb = proj(_C_GB, _C_GM)
    gbt_ref[0] = (gb * jax.nn.sigmoid(gb)).T.reshape(DIFF_HEADS, LANES, -1)[:, None].astype(bf16)

    gm_ref[0] = jax.nn.sigmoid(proj(_C_GM, _C_END) + bm_ref[...]).astype(bf16)


def _pipelined_attention(k_ref, vt_ref, s_ref, load_q, finish, *, seq, n_q, tq, n_maps):
    f32 = jnp.float32
    n_chunks = seq // KEY_CHUNK
    n_tiles = KEY_CHUNK // KEY_TILE
    assert n_chunks % 2 == 0
    rows = vt_ref.shape[2]
    neg_inf = lambda shape: jnp.full(shape, -jnp.inf, f32)

    def score_tile(q, i, key_chunk, slot, t, cmax):
        lo = key_chunk * KEY_CHUNK + t * KEY_TILE
        s = jnp.dot(k_ref[0, lo:lo + KEY_TILE, :], q, preferred_element_type=f32)
        s_ref[i, slot, t * KEY_TILE:(t + 1) * KEY_TILE, :] = s
        return jnp.maximum(cmax, jnp.max(s.reshape(KEY_TILE // 8, 8, tq), axis=0))

    def q_tile(qi, cmax):
        qs = load_q(qi)
        qs_next = load_q(jnp.minimum(qi + 1, n_q - 1))
        cmax = list(cmax)
        m = [neg_inf((1, tq)) for _ in range(n_maps)]
        acc = [jnp.zeros((rows, tq), f32) for _ in range(n_maps)]
        for c in range(n_chunks):
            m_new = [jnp.maximum(m[i], jnp.max(cmax[i], axis=0, keepdims=True)) for i in range(n_maps)]
            alpha = [jnp.exp2(m[i] - m_new[i]) for i in range(n_maps)]
            cmax_next = [neg_inf((8, tq)) for _ in range(n_maps)]
            pv = [None] * n_maps
            for t in range(n_tiles):
                for i in range(n_maps):
                    if c + 1 < n_chunks:
                        cmax_next[i] = score_tile(qs[i], i, c + 1, (c + 1) % 2, t, cmax_next[i])
                    else:
                        cmax_next[i] = score_tile(qs_next[i], i, 0, 0, t, cmax_next[i])
                    s = s_ref[i, c % 2, t * KEY_TILE:(t + 1) * KEY_TILE, :]
                    p = jnp.exp2(s - m_new[i]).astype(jnp.bfloat16)
                    lo = c * KEY_CHUNK + t * KEY_TILE
                    d = jnp.dot(vt_ref[0, 0, :, lo:lo + KEY_TILE], p, preferred_element_type=f32)
                    pv[i] = d if pv[i] is None else pv[i] + d
            acc = [alpha[i] * acc[i] + pv[i] for i in range(n_maps)]
            m, cmax = m_new, cmax_next
        finish(qi, acc)
        return tuple(cmax)

    q0 = load_q(0)
    cmax0 = [neg_inf((8, tq)) for _ in range(n_maps)]
    for t in range(n_tiles):
        for i in range(n_maps):
            cmax0[i] = score_tile(q0[i], i, 0, 0, t, cmax0[i])
    lax.fori_loop(0, n_q, q_tile, tuple(cmax0))


def _mla_attn_kernel(k_ref, qt_ref, vt_ref, gt_ref, o_ref, s_ref, *, seq, n_q):
    f32 = jnp.float32
    tq = qt_ref.shape[-1]

    def load_q(qi):
        return [qt_ref[0, 0, 2 * qi], qt_ref[0, 0, 2 * qi + 1]]

    def finish(qi, accs):
        for i, acc in enumerate(accs):
            o = acc[:MLA_V] * (1.0 / acc[MLA_V:MLA_V + 1])
            o_ref[0, 0, 2 * qi + i] = (o * gt_ref[0, 0, 2 * qi + i].astype(f32)).astype(o_ref.dtype)

    _pipelined_attention(k_ref, vt_ref, s_ref, load_q, finish, seq=seq, n_q=n_q // 2, tq=tq, n_maps=2)


def _diff_attn_kernel(k_ref, qt_ref, vt_ref, gt_ref, lam_ref, gd_ref, o_ref, s_ref, *, seq, n_q, lam_init):
    f32 = jnp.float32
    tq = qt_ref.shape[-1]
    dv = 2 * DIFF_HEAD_DIM
    lp = lam_ref[...].astype(f32)
    lam = (jnp.exp(jnp.sum(lp[0:1] * lp[1:2], axis=1, keepdims=True))
           - jnp.exp(jnp.sum(lp[2:3] * lp[3:4], axis=1, keepdims=True)) + lam_init)
    first = lax.broadcasted_iota(jnp.int32, (LANES, tq), 0) < DIFF_HEAD_DIM

    def load_q(qi):
        qt = qt_ref[0, 0, qi]
        zero = jnp.zeros_like(qt)
        return [jnp.where(first, qt, zero), jnp.where(first, zero, qt)]

    def finish(qi, accs):
        o1 = accs[0][:dv] * (1.0 / accs[0][dv:dv + 1])
        o2 = accs[1][:dv] * (1.0 / accs[1][dv:dv + 1])
        o = o1 - lam * o2
        o = o * lax.rsqrt(jnp.mean(o * o, axis=0, keepdims=True) + DIFF_RMS_EPS)
        o = o * gd_ref[...] * (1.0 - lam_init)
        o_ref[0, 0, qi] = (o * gt_ref[0, 0, qi].astype(f32)).astype(o_ref.dtype)

    _pipelined_attention(k_ref, vt_ref, s_ref, load_q, finish, seq=seq, n_q=n_q, tq=tq, n_maps=2)


def _out_kernel(x_ref, za_ref, zb_ref, gm_ref, wa_ref, wb_ref, wo_ref, lg_ref, lb_ref, o_ref):
    bf16, f32 = jnp.bfloat16, jnp.float32
    tm = x_ref.shape[1]
    za = za_ref[0].astype(f32).reshape(MLA_WIDTH, tm).T.astype(bf16)
    zb = zb_ref[0].astype(f32).reshape(DIFF_WIDTH, tm).T.astype(bf16)
    ya = jnp.dot(za, wa_ref[...], preferred_element_type=f32)
    yb = jnp.dot(zb, wb_ref[...], preferred_element_type=f32)
    gm = gm_ref[0].astype(f32)
    merged = gm[:, :D_MODEL] * ya + gm[:, D_MODEL:] * yb
    out = jnp.dot(merged.astype(bf16), wo_ref[...], preferred_element_type=f32)
    r = DEEPNORM_ALPHA * x_ref[0] + out
    mu = jnp.mean(r, axis=-1, keepdims=True)
    d = r - mu
    var = jnp.mean(d * d, axis=-1, keepdims=True)
    o_ref[0] = d * lax.rsqrt(var + LN_EPS) * lg_ref[...] + lb_ref[...]


def _const_spec(shape):
    return pl.BlockSpec(shape, lambda *_: (0,) * len(shape))


def _layer(x, p, l, tab_a, tab_b):
    bf16, f32 = jnp.bfloat16, jnp.float32
    batch, seq, _ = x.shape
    tm = TOKEN_TILE
    n_q = seq // tm
    cparams = functools.partial(pltpu.CompilerParams, vmem_limit_bytes=VMEM_LIMIT_BYTES)

    w_in = p["w_in"][l]
    offs = [0]
    for width in (MLA_Q_LORA, MLA_KV_LORA, MLA_ROPE, MLA_WIDTH, DIFF_WIDTH, DIFF_WIDTH,
                  DIFF_WIDTH, DIFF_WIDTH, 2 * D_MODEL):
        offs.append(offs[-1] + width)
    kr_cols = jnp.pad(w_in[:, offs[2]:offs[3]], ((0, 0), (MLA_NOPE, LANES - MLA_NOPE - MLA_ROPE)))
    w1 = jnp.concatenate([w_in[:, :offs[2]], kr_cols, w_in[:, offs[3]:]], axis=1).astype(bf16)
    wq = p["w_q_up"][l].reshape(MLA_Q_LORA, MLA_HEADS, MLA_NOPE + MLA_ROPE)
    wq = jnp.pad(wq, ((0, 0), (0, 0), (0, LANES - MLA_NOPE - MLA_ROPE)))
    wq = wq.reshape(MLA_Q_LORA, MLA_HEADS * LANES).astype(bf16)
    wkv = p["w_kv_up"][l].reshape(MLA_KV_LORA, MLA_HEADS, MLA_NOPE + MLA_V)
    wkn = jnp.pad(wkv[:, :, :MLA_NOPE], ((0, 0), (0, 0), (0, LANES - MLA_NOPE)))
    wkn = wkn.reshape(MLA_KV_LORA, MLA_HEADS * LANES).astype(bf16)
    wv = wkv[:, :, MLA_NOPE:].reshape(MLA_KV_LORA, MLA_WIDTH).astype(bf16)

    va_rows = MLA_V + ONES_ROWS
    vb_rows = 2 * DIFF_HEAD_DIM + ONES_ROWS
    tok = lambda b, i: (b, i, 0)
    tile5 = lambda b, i: (b, 0, i, 0, 0)
    tab_spec = pl.BlockSpec((3, tm, LANES), lambda b, i: (0, i, 0))
    ka, qat, vat, gat, kb, qbt, vbt, gbt, gm = pl.pallas_call(
        _proj_kernel,
        grid=(batch, n_q),
        in_specs=[
            pl.BlockSpec((1, tm, D_MODEL), tok),
            _const_spec(w1.shape), _const_spec(wq.shape), _const_spec(wkn.shape), _const_spec(wv.shape),
            _const_spec((1, MLA_Q_LORA)), _const_spec((1, MLA_KV_LORA)), _const_spec((1, 2 * D_MODEL)),
            tab_spec, tab_spec,
        ],
        out_specs=[
            pl.BlockSpec((1, tm, MLA_HEADS * LANES), tok),
            pl.BlockSpec((1, MLA_HEADS, 1, LANES, tm), tile5),
            pl.BlockSpec((1, MLA_HEADS, va_rows, tm), lambda b, i: (b, 0, 0, i)),
            pl.BlockSpec((1, MLA_HEADS, 1, MLA_V, tm), tile5),
            pl.BlockSpec((1, tm, DIFF_WIDTH), tok),
            pl.BlockSpec((1, DIFF_HEADS, 1, LANES, tm), tile5),
            pl.BlockSpec((1, DIFF_HEADS, vb_rows, tm), lambda b, i: (b, 0, 0, i)),
            pl.BlockSpec((1, DIFF_HEADS, 1, LANES, tm), tile5),
            pl.BlockSpec((1, tm, 2 * D_MODEL), tok),
        ],
        out_shape=[
            jax.ShapeDtypeStruct((batch, seq, MLA_HEADS * LANES), bf16),
            jax.ShapeDtypeStruct((batch, MLA_HEADS, n_q, LANES, tm), bf16),
            jax.ShapeDtypeStruct((batch, MLA_HEADS, va_rows, seq), bf16),
            jax.ShapeDtypeStruct((batch, MLA_HEADS, n_q, MLA_V, tm), bf16),
            jax.ShapeDtypeStruct((batch, seq, DIFF_WIDTH), bf16),
            jax.ShapeDtypeStruct((batch, DIFF_HEADS, n_q, LANES, tm), bf16),
            jax.ShapeDtypeStruct((batch, DIFF_HEADS, vb_rows, seq), bf16),
            jax.ShapeDtypeStruct((batch, DIFF_HEADS, n_q, LANES, tm), bf16),
            jax.ShapeDtypeStruct((batch, seq, 2 * D_MODEL), bf16),
        ],
        compiler_params=cparams(dimension_semantics=("arbitrary", "arbitrary")),
        name="proj",
    )(x, w1, wq, wkn, wv, p["g_q"][l][None], p["g_kv"][l][None], p["b_merge"][l][None], tab_a, tab_b)

    head4 = lambda b, h: (b, h, 0, 0)
    head5 = lambda b, h: (b, h, 0, 0, 0)
    za = pl.pallas_call(
        functools.partial(_mla_attn_kernel, seq=seq, n_q=n_q),
        grid=(batch, MLA_HEADS),
        in_specs=[
            pl.BlockSpec((1, seq, LANES), lambda b, h: (b, 0, h)),
            pl.BlockSpec((1, 1, n_q, LANES, tm), head5),
            pl.BlockSpec((1, 1, va_rows, seq), head4),
            pl.BlockSpec((1, 1, n_q, MLA_V, tm), head5),
        ],
        out_specs=pl.BlockSpec((1, 1, n_q, MLA_V, tm), head5),
        out_shape=jax.ShapeDtypeStruct((batch, MLA_HEADS, n_q, MLA_V, tm), bf16),
        scratch_shapes=[pltpu.VMEM((2, 2, KEY_CHUNK, tm), f32)],
        compiler_params=cparams(dimension_semantics=("arbitrary", "arbitrary")),
        name="mla_attn",
    )(ka, qat, vat, gat)

    lam_init = 0.8 - 0.6 * math.exp(-0.3 * l)
    zb = pl.pallas_call(
        functools.partial(_diff_attn_kernel, seq=seq, n_q=n_q, lam_init=lam_init),
        grid=(batch, DIFF_HEADS),
        in_specs=[
            pl.BlockSpec((1, seq, LANES), lambda b, h: (b, 0, h)),
            pl.BlockSpec((1, 1, n_q, LANES, tm), head5),
            pl.BlockSpec((1, 1, vb_rows, seq), head4),
            pl.BlockSpec((1, 1, n_q, LANES, tm), head5),
            _const_spec((4, DIFF_HEAD_DIM)),
            _const_spec((2 * DIFF_HEAD_DIM, 1)),
        ],
        out_specs=pl.BlockSpec((1, 1, n_q, LANES, tm), head5),
        out_shape=jax.ShapeDtypeStruct((batch, DIFF_HEADS, n_q, LANES, tm), bf16),
        scratch_shapes=[pltpu.VMEM((2, 2, KEY_CHUNK, tm), f32)],
        compiler_params=cparams(dimension_semantics=("arbitrary", "arbitrary")),
        name="diff_attn",
    )(kb, qbt, vbt, gbt, p["diff_lambda"][l], p["g_diff"][l][:, None])

    return pl.pallas_call(
        _out_kernel,
        grid=(batch, n_q),
        in_specs=[
            pl.BlockSpec((1, tm, D_MODEL), tok),
            pl.BlockSpec((1, MLA_HEADS, 1, MLA_V, tm), tile5),
            pl.BlockSpec((1, DIFF_HEADS, 1, LANES, tm), tile5),
            pl.BlockSpec((1, tm, 2 * D_MODEL), tok),
            _const_spec((MLA_WIDTH, D_MODEL)), _const_spec((DIFF_WIDTH, D_MODEL)),
            _const_spec((D_MODEL, D_MODEL)),
            _const_spec((1, D_MODEL)), _const_spec((1, D_MODEL)),
        ],
        out_specs=pl.BlockSpec((1, tm, D_MODEL), tok),
        out_shape=jax.ShapeDtypeStruct((batch, seq, D_MODEL), f32),
        compiler_params=cparams(dimension_semantics=("arbitrary", "arbitrary")),
        name="out_proj",
    )(x, za, zb, gm, p["w_branch_a"][l].astype(bf16), p["w_branch_b"][l].astype(bf16),
      p["w_out"][l].astype(bf16), p["ln_gamma"][l][None], p["ln_beta"][l][None])


def kernel(x, w_in, g_q, w_q_up, g_kv, w_kv_up, diff_lambda, g_diff, w_branch_a, w_branch_b,
           b_merge, w_out, ln_gamma, ln_beta):
    params = dict(w_in=w_in, g_q=g_q, w_q_up=w_q_up, g_kv=g_kv, w_kv_up=w_kv_up,
                  diff_lambda=diff_lambda, g_diff=g_diff, w_branch_a=w_branch_a,
                  w_branch_b=w_branch_b, b_merge=b_merge, w_out=w_out,
                  ln_gamma=ln_gamma, ln_beta=ln_beta)
    seq = x.shape[1]
    tab_a = _rotary_tables(seq, MLA_ROPE, MLA_NOPE, LANES)
    tab_b = _rotary_tables(seq, DIFF_ROT, 0, DIFF_HEAD_DIM)
    for l in range(DEPTH):
        x = _layer(x, params, l, tab_a, tab_b)
    return x
```

```python
import functools
import math

import jax
import jax.numpy as jnp
from jax import lax
from jax.experimental import pallas as pl
from jax.experimental.pallas import tpu as pltpu

D_MODEL = 1024
DEPTH = 2
MLA_HEADS = 8
MLA_Q_LORA = 256
MLA_KV_LORA = 128
MLA_NOPE = 64
MLA_ROPE = 32
MLA_V = 64
MLA_WIDTH = MLA_HEADS * MLA_V
DIFF_HEADS = 4
DIFF_HEAD_DIM = 64
DIFF_WIDTH = DIFF_HEADS * 2 * DIFF_HEAD_DIM
DIFF_ROT = DIFF_HEAD_DIM // 4
ROPE_THETA = 500000.0
DEEPNORM_ALPHA = (2 * DEPTH) ** 0.25
LN_EPS = 1e-5
RMS_EPS = 1e-6
DIFF_RMS_EPS = 1e-5
LOG2_E = math.log2(math.e)

LANES = 128
BF16_SUBLANES = 16
TOKEN_TILE = 256
KEY_CHUNK = 512
KEY_TILE = 256
ONES_ROWS = BF16_SUBLANES
MLA_TILES_PER_STEP = 4
DIFF_TILES_PER_STEP = 2
VMEM_LIMIT_BYTES = 56 * 1024 * 1024
_C_Q = 0
_C_KV = _C_Q + MLA_Q_LORA
_C_KR = _C_KV + MLA_KV_LORA
_C_GA = _C_KR + LANES
_C_QD = _C_GA + MLA_WIDTH
_C_KD = _C_QD + DIFF_WIDTH
_C_VD = _C_KD + DIFF_WIDTH
_C_GB = _C_VD + DIFF_WIDTH
_C_GM = _C_GB + DIFF_WIDTH
_C_END = _C_GM + 2 * D_MODEL


def _rotary_tables(seq, rot_dim, first_lane, period):
    half = rot_dim // 2
    inv_freq = ROPE_THETA ** (-jnp.arange(half, dtype=jnp.float32) / half)
    ang = jnp.arange(seq, dtype=jnp.float32)[:, None] * inv_freq[None, :]
    cos, sin = jnp.cos(ang), jnp.sin(ang)
    lane = jnp.arange(LANES) % period - first_lane
    lo = (lane >= 0) & (lane < half)
    hi = (lane >= half) & (lane < rot_dim)
    idx = jnp.clip(jnp.where(hi, lane - half, lane), 0, half - 1)
    cos_l, sin_l = cos[:, idx], sin[:, idx]
    c = jnp.where((lo | hi)[None, :], cos_l, 1.0)
    s_lo = jnp.where(lo[None, :], -sin_l, 0.0)
    s_hi = jnp.where(hi[None, :], sin_l, 0.0)
    return jnp.stack([c, s_lo, s_hi]).astype(jnp.float32)


def _rotate(x, tab_ref, half):
    up = pltpu.roll(x, LANES - half, 1)
    down = pltpu.roll(x, half, 1)
    return x * tab_ref[0] + up * tab_ref[1] + down * tab_ref[2]


def _rms_norm(x, g, eps):
    return x * lax.rsqrt(jnp.mean(x * x, axis=-1, keepdims=True) + eps) * g


def _with_ones_rows(vt):
    heads, _, tokens = vt.shape
    return jnp.concatenate([vt, jnp.ones((heads, ONES_ROWS, tokens), vt.dtype)], axis=1)


def _proj_kernel(x_ref, w1_ref, wq_ref, wkn_ref, wv_ref, gq_ref, gkv_ref, bm_ref,
                 taba_ref, tabb_ref,
                 ka_ref, qat_ref, vat_ref, gat_ref, kb_ref, qbt_ref, vbt_ref, gbt_ref, gm_ref):
    bf16, f32 = jnp.bfloat16, jnp.float32
    xb = x_ref[0].astype(bf16)

    def proj(lo, hi):
        return jnp.dot(xb, w1_ref[:, lo:hi], preferred_element_type=f32)

    lat = proj(_C_Q, _C_GA)
    cq = _rms_norm(lat[:, _C_Q:_C_KV], gq_ref[...], RMS_EPS).astype(bf16)
    ckv = _rms_norm(lat[:, _C_KV:_C_KR], gkv_ref[...], RMS_EPS).astype(bf16)
    k_rope = _rotate(lat[:, _C_KR:_C_GA], taba_ref, MLA_ROPE // 2)
    mla_scale = LOG2_E / math.sqrt(MLA_NOPE + MLA_ROPE)
    q_all = jnp.dot(cq, wq_ref[...], preferred_element_type=f32)
    k_all = jnp.dot(ckv, wkn_ref[...], preferred_element_type=f32)
    for h in range(MLA_HEADS):
        sl = slice(h * LANES, (h + 1) * LANES)
        q_h = _rotate(q_all[:, sl], taba_ref, MLA_ROPE // 2) * mla_scale
        qat_ref[0, h, 0] = q_h.T.astype(bf16)
        ka_ref[0, :, sl] = (k_all[:, sl] + k_rope).astype(bf16)
    v = jnp.dot(ckv, wv_ref[...], preferred_element_type=f32)
    vat_ref[0] = _with_ones_rows(v.T.reshape(MLA_HEADS, MLA_V, -1)).astype(bf16)
    ga = proj(_C_GA, _C_QD)
    gat_ref[0] = (ga * jax.nn.sigmoid(ga)).T.reshape(MLA_HEADS, MLA_V, -1)[:, None].astype(bf16)

    diff_scale = LOG2_E / math.sqrt(DIFF_HEAD_DIM)
    qk_d = proj(_C_QD, _C_VD)
    for h in range(DIFF_HEADS):
        sl = slice(h * LANES, (h + 1) * LANES)
        q_h = _rotate(qk_d[:, sl], tabb_ref, DIFF_ROT // 2)
        qbt_ref[0, h, 0] = (q_h * diff_scale).T.astype(bf16)
        k_h = _rotate(qk_d[:, DIFF_WIDTH + h * LANES:DIFF_WIDTH + (h + 1) * LANES], tabb_ref, DIFF_ROT // 2)
        kb_ref[0, :, sl] = k_h.astype(bf16)
    vbt_ref[0] = _with_ones_rows(proj(_C_VD, _C_GB).T.reshape(DIFF_HEADS, LANES, -1)).astype(bf16)
    gb = proj(_C_GB, _C_GM)
    gbt_ref[0] = (gb * jax.nn.sigmoid(gb)).T.reshape(DIFF_HEADS, LANES, -1)[:, None].astype(bf16)

    gm_ref[0] = jax.nn.sigmoid(proj(_C_GM, _C_END) + bm_ref[...]).astype(bf16)


def _pipelined_attention(k_ref, vt_ref, s_ref, load_q, finish, *, seq, n_q, tq, n_maps):
    f32 = jnp.float32
    n_chunks = seq // KEY_CHUNK
    n_tiles = KEY_CHUNK // KEY_TILE
    assert n_chunks % 2 == 0
    rows = vt_ref.shape[2]
    neg_inf = lambda shape: jnp.full(shape, -jnp.inf, f32)

    def score_tile(q, i, key_chunk, slot, t, cmax):
        lo = key_chunk * KEY_CHUNK + t * KEY_TILE
        s = jnp.dot(k_ref[0, lo:lo + KEY_TILE, :], q, preferred_element_type=f32)
        s_ref[i, slot, t * KEY_TILE:(t + 1) * KEY_TILE, :] = s
        return jnp.maximum(cmax, jnp.max(s.reshape(KEY_TILE // 8, 8, tq), axis=0))

    def q_tile(qi, cmax):
        qs = load_q(qi)
        qs_next = load_q(jnp.minimum(qi + 1, n_q - 1))
        cmax = list(cmax)
        m = [neg_inf((1, tq)) for _ in range(n_maps)]
        acc = [jnp.zeros((rows, tq), f32) for _ in range(n_maps)]
        pending = [None] * n_maps
        pv = [None] * n_maps

        def issue_pv(i):
            nonlocal acc
            p, lo, alpha_c, last = pending[i]
            d = jnp.dot(vt_ref[0, 0, :, lo:lo + KEY_TILE], p, preferred_element_type=f32)
            pv[i] = d if pv[i] is None else pv[i] + d
            if last:
                acc[i] = alpha_c * acc[i] + pv[i]
                pv[i] = None
            pending[i] = None

        for c in range(n_chunks):
            m_new = [jnp.maximum(m[i], jnp.max(cmax[i], axis=0, keepdims=True)) for i in range(n_maps)]
            alpha = [jnp.exp2(m[i] - m_new[i]) for i in range(n_maps)]
            cmax_next = [neg_inf((8, tq)) for _ in range(n_maps)]
            for t in range(n_tiles):
                for i in range(n_maps):
                    if c + 1 < n_chunks:
                        cmax_next[i] = score_tile(qs[i], i, c + 1, (c + 1) % 2, t, cmax_next[i])
                    else:
                        cmax_next[i] = score_tile(qs_next[i], i, 0, 0, t, cmax_next[i])
                    s = s_ref[i, c % 2, t * KEY_TILE:(t + 1) * KEY_TILE, :]
                    p = jnp.exp2(s - m_new[i]).astype(jnp.bfloat16)
                    if pending[i] is not None:
                        issue_pv(i)
                    pending[i] = (p, c * KEY_CHUNK + t * KEY_TILE, alpha[i], t == n_tiles - 1)
            m, cmax = m_new, cmax_next
        for i in range(n_maps):
            issue_pv(i)
        finish(qi, acc)
        return tuple(cmax)

    q0 = load_q(0)
    cmax0 = [neg_inf((8, tq)) for _ in range(n_maps)]
    for t in range(n_tiles):
        for i in range(n_maps):
            cmax0[i] = score_tile(q0[i], i, 0, 0, t, cmax0[i])
    lax.fori_loop(0, n_q, q_tile, tuple(cmax0))


def _mla_attn_kernel(k_ref, qt_ref, vt_ref, gt_ref, o_ref, s_ref, *, seq, n_q):
    f32 = jnp.float32
    tq = qt_ref.shape[-1]

    per_step = MLA_TILES_PER_STEP

    def load_q(qi):
        return [qt_ref[0, 0, per_step * qi + j] for j in range(per_step)]

    def finish(qi, accs):
        for j, acc in enumerate(accs):
            o = acc[:MLA_V] * (1.0 / acc[MLA_V:MLA_V + 1])
            tile = per_step * qi + j
            o_ref[0, 0, tile] = (o * gt_ref[0, 0, tile].astype(f32)).astype(o_ref.dtype)

    _pipelined_attention(k_ref, vt_ref, s_ref, load_q, finish, seq=seq, n_q=n_q // per_step, tq=tq,
                         n_maps=per_step)


def _diff_attn_kernel(k_ref, qt_ref, vt_ref, gt_ref, lam_ref, gd_ref, o_ref, s_ref, *, seq, n_q, lam_init):
    f32 = jnp.float32
    tq = qt_ref.shape[-1]
    dv = 2 * DIFF_HEAD_DIM
    lp = lam_ref[...].astype(f32)
    lam = (jnp.exp(jnp.sum(lp[0:1] * lp[1:2], axis=1, keepdims=True))
           - jnp.exp(jnp.sum(lp[2:3] * lp[3:4], axis=1, keepdims=True)) + lam_init)
    first = lax.broadcasted_iota(jnp.int32, (LANES, tq), 0) < DIFF_HEAD_DIM

    per_step = DIFF_TILES_PER_STEP

    def load_q(qi):
        maps = []
        for j in range(per_step):
            qt = qt_ref[0, 0, per_step * qi + j]
            zero = jnp.zeros_like(qt)
            maps += [jnp.where(first, qt, zero), jnp.where(first, zero, qt)]
        return maps

    def finish(qi, accs):
        for j in range(per_step):
            a1, a2 = accs[2 * j], accs[2 * j + 1]
            o = a1[:dv] * (1.0 / a1[dv:dv + 1]) - lam * (a2[:dv] * (1.0 / a2[dv:dv + 1]))
            o = o * lax.rsqrt(jnp.mean(o * o, axis=0, keepdims=True) + DIFF_RMS_EPS)
            o = o * gd_ref[...] * (1.0 - lam_init)
            tile = per_step * qi + j
            o_ref[0, 0, tile] = (o * gt_ref[0, 0, tile].astype(f32)).astype(o_ref.dtype)

    _pipelined_attention(k_ref, vt_ref, s_ref, load_q, finish, seq=seq, n_q=n_q // per_step, tq=tq,
                         n_maps=2 * per_step)


def _out_kernel(x_ref, za_ref, zb_ref, gm_ref, wa_ref, wb_ref, wo_ref, lg_ref, lb_ref, o_ref):
    bf16, f32 = jnp.bfloat16, jnp.float32
    tm = x_ref.shape[1]
    za = za_ref[0].astype(f32).reshape(MLA_WIDTH, tm).T.astype(bf16)
    zb = zb_ref[0].astype(f32).reshape(DIFF_WIDTH, tm).T.astype(bf16)
    ya = jnp.dot(za, wa_ref[...], preferred_element_type=f32)
    yb = jnp.dot(zb, wb_ref[...], preferred_element_type=f32)
    gm = gm_ref[0].astype(f32)
    merged = gm[:, :D_MODEL] * ya + gm[:, D_MODEL:] * yb
    out = jnp.dot(merged.astype(bf16), wo_ref[...], preferred_element_type=f32)
    r = DEEPNORM_ALPHA * x_ref[0] + out
    mu = jnp.mean(r, axis=-1, keepdims=True)
    d = r - mu
    var = jnp.mean(d * d, axis=-1, keepdims=True)
    o_ref[0] = d * lax.rsqrt(var + LN_EPS) * lg_ref[...] + lb_ref[...]


def _const_spec(shape):
    return pl.BlockSpec(shape, lambda *_: (0,) * len(shape))


def _layer(x, p, l, tab_a, tab_b):
    bf16, f32 = jnp.bfloat16, jnp.float32
    batch, seq, _ = x.shape
    tm = TOKEN_TILE
    n_q = seq // tm
    cparams = functools.partial(pltpu.CompilerParams, vmem_limit_bytes=VMEM_LIMIT_BYTES)

    w_in = p["w_in"][l]
    offs = [0]
    for width in (MLA_Q_LORA, MLA_KV_LORA, MLA_ROPE, MLA_WIDTH, DIFF_WIDTH, DIFF_WIDTH,
                  DIFF_WIDTH, DIFF_WIDTH, 2 * D_MODEL):
        offs.append(offs[-1] + width)
    kr_cols = jnp.pad(w_in[:, offs[2]:offs[3]], ((0, 0), (MLA_NOPE, LANES - MLA_NOPE - MLA_ROPE)))
    w1 = jnp.concatenate([w_in[:, :offs[2]], kr_cols, w_in[:, offs[3]:]], axis=1).astype(bf16)
    wq = p["w_q_up"][l].reshape(MLA_Q_LORA, MLA_HEADS, MLA_NOPE + MLA_ROPE)
    wq = jnp.pad(wq, ((0, 0), (0, 0), (0, LANES - MLA_NOPE - MLA_ROPE)))
    wq = wq.reshape(MLA_Q_LORA, MLA_HEADS * LANES).astype(bf16)
    wkv = p["w_kv_up"][l].reshape(MLA_KV_LORA, MLA_HEADS, MLA_NOPE + MLA_V)
    wkn = jnp.pad(wkv[:, :, :MLA_NOPE], ((0, 0), (0, 0), (0, LANES - MLA_NOPE)))
    wkn = wkn.reshape(MLA_KV_LORA, MLA_HEADS * LANES).astype(bf16)
    wv = wkv[:, :, MLA_NOPE:].reshape(MLA_KV_LORA, MLA_WIDTH).astype(bf16)

    va_rows = MLA_V + ONES_ROWS
    vb_rows = 2 * DIFF_HEAD_DIM + ONES_ROWS
    tok = lambda b, i: (b, i, 0)
    tile5 = lambda b, i: (b, 0, i, 0, 0)
    tab_spec = pl.BlockSpec((3, tm, LANES), lambda b, i: (0, i, 0))
    ka, qat, vat, gat, kb, qbt, vbt, gbt, gm = pl.pallas_call(
        _proj_kernel,
        grid=(batch, n_q),
        in_specs=[
            pl.BlockSpec((1, tm, D_MODEL), tok),
            _const_spec(w1.shape), _const_spec(wq.shape), _const_spec(wkn.shape), _const_spec(wv.shape),
            _const_spec((1, MLA_Q_LORA)), _const_spec((1, MLA_KV_LORA)), _const_spec((1, 2 * D_MODEL)),
            tab_spec, tab_spec,
        ],
        out_specs=[
            pl.BlockSpec((1, tm, MLA_HEADS * LANES), tok),
            pl.BlockSpec((1, MLA_HEADS, 1, LANES, tm), tile5),
            pl.BlockSpec((1, MLA_HEADS, va_rows, tm), lambda b, i: (b, 0, 0, i)),
            pl.BlockSpec((1, MLA_HEADS, 1, MLA_V, tm), tile5),
            pl.BlockSpec((1, tm, DIFF_WIDTH), tok),
            pl.BlockSpec((1, DIFF_HEADS, 1, LANES, tm), tile5),
            pl.BlockSpec((1, DIFF_HEADS, vb_rows, tm), lambda b, i: (b, 0, 0, i)),
            pl.BlockSpec((1, DIFF_HEADS, 1, LANES, tm), tile5),
            pl.BlockSpec((1, tm, 2 * D_MODEL), tok),
        ],
        out_shape=[
            jax.ShapeDtypeStruct((batch, seq, MLA_HEADS * LANES), bf16),
            jax.ShapeDtypeStruct((batch, MLA_HEADS, n_q, LANES, tm), bf16),
            jax.ShapeDtypeStruct((batch, MLA_HEADS, va_rows, seq), bf16),
            jax.ShapeDtypeStruct((batch, MLA_HEADS, n_q, MLA_V, tm), bf16),
            jax.ShapeDtypeStruct((batch, seq, DIFF_WIDTH), bf16),
            jax.ShapeDtypeStruct((batch, DIFF_HEADS, n_q, LANES, tm), bf16),
            jax.ShapeDtypeStruct((batch, DIFF_HEADS, vb_rows, seq), bf16),
            jax.ShapeDtypeStruct((batch, DIFF_HEADS, n_q, LANES, tm), bf16),
            jax.ShapeDtypeStruct((batch, seq, 2 * D_MODEL), bf16),
        ],
        compiler_params=cparams(dimension_semantics=("arbitrary", "arbitrary")),
        name="proj",
    )(x, w1, wq, wkn, wv, p["g_q"][l][None], p["g_kv"][l][None], p["b_merge"][l][None], tab_a, tab_b)

    head4 = lambda b, h: (b, h, 0, 0)
    head5 = lambda b, h: (b, h, 0, 0, 0)
    za = pl.pallas_call(
        functools.partial(_mla_attn_kernel, seq=seq, n_q=n_q),
        grid=(batch, MLA_HEADS),
        in_specs=[
            pl.BlockSpec((1, seq, LANES), lambda b, h: (b, 0, h)),
            pl.BlockSpec((1, 1, n_q, LANES, tm), head5),
            pl.BlockSpec((1, 1, va_rows, seq), head4),
            pl.BlockSpec((1, 1, n_q, MLA_V, tm), head5),
        ],
        out_specs=pl.BlockSpec((1, 1, n_q, MLA_V, tm), head5),
        out_shape=jax.ShapeDtypeStruct((batch, MLA_HEADS, n_q, MLA_V, tm), bf16),
        scratch_shapes=[pltpu.VMEM((MLA_TILES_PER_STEP, 2, KEY_CHUNK, tm), f32)],
        compiler_params=cparams(dimension_semantics=("arbitrary", "arbitrary")),
        name="mla_attn",
    )(ka, qat, vat, gat)

    lam_init = 0.8 - 0.6 * math.exp(-0.3 * l)
    zb = pl.pallas_call(
        functools.partial(_diff_attn_kernel, seq=seq, n_q=n_q, lam_init=lam_init),
        grid=(batch, DIFF_HEADS),
        in_specs=[
            pl.BlockSpec((1, seq, LANES), lambda b, h: (b, 0, h)),
            pl.BlockSpec((1, 1, n_q, LANES, tm), head5),
            pl.BlockSpec((1, 1, vb_rows, seq), head4),
            pl.BlockSpec((1, 1, n_q, LANES, tm), head5),
            _const_spec((4, DIFF_HEAD_DIM)),
            _const_spec((2 * DIFF_HEAD_DIM, 1)),
        ],
        out_specs=pl.BlockSpec((1, 1, n_q, LANES, tm), head5),
        out_shape=jax.ShapeDtypeStruct((batch, DIFF_HEADS, n_q, LANES, tm), bf16),
        scratch_shapes=[pltpu.VMEM((2 * DIFF_TILES_PER_STEP, 2, KEY_CHUNK, tm), f32)],
        compiler_params=cparams(dimension_semantics=("arbitrary", "arbitrary")),
        name="diff_attn",
    )(kb, qbt, vbt, gbt, p["diff_lambda"][l], p["g_diff"][l][:, None])

    return pl.pallas_call(
        _out_kernel,
        grid=(batch, n_q),
        in_specs=[
            pl.BlockSpec((1, tm, D_MODEL), tok),
            pl.BlockSpec((1, MLA_HEADS, 1, MLA_V, tm), tile5),
            pl.BlockSpec((1, DIFF_HEADS, 1, LANES, tm), tile5),
            pl.BlockSpec((1, tm, 2 * D_MODEL), tok),
            _const_spec((MLA_WIDTH, D_MODEL)), _const_spec((DIFF_WIDTH, D_MODEL)),
            _const_spec((D_MODEL, D_MODEL)),
            _const_spec((1, D_MODEL)), _const_spec((1, D_MODEL)),
        ],
        out_specs=pl.BlockSpec((1, tm, D_MODEL), tok),
        out_shape=jax.ShapeDtypeStruct((batch, seq, D_MODEL), f32),
        compiler_params=cparams(dimension_semantics=("arbitrary", "arbitrary")),
        name="out_proj",
    )(x, za, zb, gm, p["w_branch_a"][l].astype(bf16), p["w_branch_b"][l].astype(bf16),
      p["w_out"][l].astype(bf16), p["ln_gamma"][l][None], p["ln_beta"][l][None])


def kernel(x, w_in, g_q, w_q_up, g_kv, w_kv_up, diff_lambda, g_diff, w_branch_a, w_branch_b,
           b_merge, w_out, ln_gamma, ln_beta):
    params = dict(w_in=w_in, g_q=g_q, w_q_up=w_q_up, g_kv=g_kv, w_kv_up=w_kv_up,
                  diff_lambda=diff_lambda, g_diff=g_diff, w_branch_a=w_branch_a,
                  w_branch_b=w_branch_b, b_merge=b_merge, w_out=w_out,
                  ln_gamma=ln_gamma, ln_beta=ln_beta)
    seq = x.shape[1]
    tab_a = _rotary_tables(seq, MLA_ROPE, MLA_NOPE, LANES)
    tab_b = _rotary_tables(seq, DIFF_ROT, 0, DIFF_HEAD_DIM)
    for l in range(DEPTH):
        x = _layer(x, params, l, tab_a, tab_b)
    return x
```

```python
import functools
import math

import jax
import jax.numpy as jnp
from jax import lax
from jax.experimental import pallas as pl
from jax.experimental.pallas import tpu as pltpu

D_MODEL = 1024
DEPTH = 2
MLA_HEADS = 8
MLA_Q_LORA = 256
MLA_KV_LORA = 128
MLA_NOPE = 64
MLA_ROPE = 32
MLA_V = 64
MLA_WIDTH = MLA_HEADS * MLA_V
DIFF_HEADS = 4
DIFF_HEAD_DIM = 64
DIFF_WIDTH = DIFF_HEADS * 2 * DIFF_HEAD_DIM
DIFF_ROT = DIFF_HEAD_DIM // 4
ROPE_THETA = 500000.0
DEEPNORM_ALPHA = (2 * DEPTH) ** 0.25
LN_EPS = 1e-5
RMS_EPS = 1e-6
DIFF_RMS_EPS = 1e-5
LOG2_E = math.log2(math.e)

LANES = 128
BF16_SUBLANES = 16
TOKEN_TILE = 256
STEP_TILES = 2
KEY_CHUNK = 256
KEY_TILE = 256
ONES_ROWS = BF16_SUBLANES
MLA_TILES_PER_STEP = 4
DIFF_TILES_PER_STEP = 2
VMEM_LIMIT_BYTES = 56 * 1024 * 1024

_IN_KR = MLA_Q_LORA + MLA_KV_LORA
_IN_REST = _IN_KR + MLA_ROPE
_L_KV = MLA_Q_LORA
_L_KR = _L_KV + MLA_KV_LORA
_L_END = _L_KR + LANES
_R_GA = 0
_R_QD = _R_GA + MLA_WIDTH
_R_KD = _R_QD + DIFF_WIDTH
_R_VD = _R_KD + DIFF_WIDTH
_R_GB = _R_VD + DIFF_WIDTH
_R_GM = _R_GB + DIFF_WIDTH
_R_END = _R_GM + 2 * D_MODEL


def _rotary_tables(seq, rot_dim, first_lane, period):
    half = rot_dim // 2
    inv_freq = ROPE_THETA ** (-jnp.arange(half, dtype=jnp.float32) / half)
    ang = jnp.arange(seq, dtype=jnp.float32)[:, None] * inv_freq[None, :]
    cos, sin = jnp.cos(ang), jnp.sin(ang)
    lane = jnp.arange(LANES) % period - first_lane
    lo = (lane >= 0) & (lane < half)
    hi = (lane >= half) & (lane < rot_dim)
    idx = jnp.clip(jnp.where(hi, lane - half, lane), 0, half - 1)
    cos_l, sin_l = cos[:, idx], sin[:, idx]
    c = jnp.where((lo | hi)[None, :], cos_l, 1.0)
    s_lo = jnp.where(lo[None, :], -sin_l, 0.0)
    s_hi = jnp.where(hi[None, :], sin_l, 0.0)
    return jnp.stack([c, s_lo, s_hi]).astype(jnp.float32)


def _rotate(x, tab, half):
    up = pltpu.roll(x, LANES - half, 1)
    down = pltpu.roll(x, half, 1)
    return x * tab[0] + up * tab[1] + down * tab[2]


def _rms_norm(x, g, eps):
    return x * lax.rsqrt(jnp.mean(x * x, axis=-1, keepdims=True) + eps) * g


def _with_ones_rows(vt):
    heads, _, tokens = vt.shape
    return jnp.concatenate([vt, jnp.ones((heads, ONES_ROWS, tokens), vt.dtype)], axis=1)


def _proj_kernel(x_ref, wlat_ref, wrest_ref, wq_ref, wkn_ref, wv_ref, gq_ref, gkv_ref, bm_ref,
                 taba_ref, tabb_ref,
                 ka_ref, qat_ref, vat_ref, gat_ref, kb_ref, qbt_ref, vbt_ref, gbt_ref, gm_ref):
    bf16, f32 = jnp.bfloat16, jnp.float32
    tm = TOKEN_TILE
    mla_scale = LOG2_E / math.sqrt(MLA_NOPE + MLA_ROPE)
    diff_scale = LOG2_E / math.sqrt(DIFF_HEAD_DIM)

    for j in range(STEP_TILES):
        rows = slice(j * tm, (j + 1) * tm)
        xb = x_ref[0, rows, :].astype(bf16)
        tab_a = [taba_ref[i, rows, :] for i in range(3)]
        tab_b = [tabb_ref[i, rows, :] for i in range(3)]

        def proj(lo, hi):
            return jnp.dot(xb, wrest_ref[:, lo:hi], preferred_element_type=f32)

        lat = jnp.dot(xb, wlat_ref[...], preferred_element_type=f32)
        cq = _rms_norm(lat[:, :_L_KV], gq_ref[...], RMS_EPS).astype(bf16)
        ckv = _rms_norm(lat[:, _L_KV:_L_KR], gkv_ref[...], RMS_EPS).astype(bf16)
        k_rope = _rotate(lat[:, _L_KR:_L_END], tab_a, MLA_ROPE // 2)
        q_all = jnp.dot(cq, wq_ref[...], preferred_element_type=f32)
        k_all = jnp.dot(ckv, wkn_ref[...], preferred_element_type=f32)
        for h in range(MLA_HEADS):
            sl = slice(h * LANES, (h + 1) * LANES)
            q_h = _rotate(q_all[:, sl], tab_a, MLA_ROPE // 2) * mla_scale
            qat_ref[0, h, j] = q_h.T.astype(bf16)
            ka_ref[0, rows, sl] = (k_all[:, sl] + k_rope).astype(bf16)
        v = jnp.dot(ckv, wv_ref[...], preferred_element_type=f32)
        vat_ref[0, :, :, rows] = _with_ones_rows(v.T.reshape(MLA_HEADS, MLA_V, tm)).astype(bf16)
        ga = proj(_R_GA, _R_QD)
        gat_ref[0, :, j] = (ga * jax.nn.sigmoid(ga)).T.reshape(MLA_HEADS, MLA_V, tm).astype(bf16)

        qk_d = proj(_R_QD, _R_VD)
        for h in range(DIFF_HEADS):
            sl = slice(h * LANES, (h + 1) * LANES)
            q_h = _rotate(qk_d[:, sl], tab_b, DIFF_ROT // 2)
            qbt_ref[0, h, j] = (q_h * diff_scale).T.astype(bf16)
            k_h = _rotate(qk_d[:, DIFF_WIDTH + h * LANES:DIFF_WIDTH + (h + 1) * LANES], tab_b, DIFF_ROT // 2)
            kb_ref[0, rows, sl] = k_h.astype(bf16)
        vbt_ref[0, :, :, rows] = _with_ones_rows(proj(_R_VD, _R_GB).T.reshape(DIFF_HEADS, LANES, tm)).astype(bf16)
        gb = proj(_R_GB, _R_GM)
        gbt_ref[0, :, j] = (gb * jax.nn.sigmoid(gb)).T.reshape(DIFF_HEADS, LANES, tm).astype(bf16)

        gm_ref[0, rows, :] = jax.nn.sigmoid(proj(_R_GM, _R_END) + bm_ref[...]).astype(bf16)


def _pipelined_attention(k_ref, vt_ref, s_ref, load_q, finish, *, seq, n_q, tq, n_maps):
    f32 = jnp.float32
    n_chunks = seq // KEY_CHUNK
    n_tiles = KEY_CHUNK // KEY_TILE
    assert n_chunks % 2 == 0
    rows = vt_ref.shape[2]
    neg_inf = lambda shape: jnp.full(shape, -jnp.inf, f32)

    def score_tile(q, i, key_chunk, slot, t, cmax):
        lo = key_chunk * KEY_CHUNK + t * KEY_TILE
        s = jnp.dot(k_ref[0, lo:lo + KEY_TILE, :], q, preferred_element_type=f32)
        s_ref[i, slot, t * KEY_TILE:(t + 1) * KEY_TILE, :] = s
        return jnp.maximum(cmax, jnp.max(s.reshape(KEY_TILE // 8, 8, tq), axis=0))

    def q_tile(qi, cmax):
        qs = load_q(qi)
        qs_next = load_q(jnp.minimum(qi + 1, n_q - 1))
        cmax = list(cmax)
        m = [neg_inf((1, tq)) for _ in range(n_maps)]
        acc = [jnp.zeros((rows, tq), f32) for _ in range(n_maps)]
        pending = [None] * n_maps
        pv = [None] * n_maps

        def issue_pv(i):
            p, lo, alpha_c, last = pending[i]
            d = jnp.dot(vt_ref[0, 0, :, lo:lo + KEY_TILE], p, preferred_element_type=f32)
            pv[i] = d if pv[i] is None else pv[i] + d
            if last:
                acc[i] = alpha_c * acc[i] + pv[i]
                pv[i] = None
            pending[i] = None

        for c in range(n_chunks):
            m_new = [jnp.maximum(m[i], jnp.max(cmax[i], axis=0, keepdims=True)) for i in range(n_maps)]
            alpha = [jnp.exp2(m[i] - m_new[i]) for i in range(n_maps)]
            cmax_next = [neg_inf((8, tq)) for _ in range(n_maps)]
            for t in range(n_tiles):
                for i in range(n_maps):
                    if c + 1 < n_chunks:
                        cmax_next[i] = score_tile(qs[i], i, c + 1, (c + 1) % 2, t, cmax_next[i])
                    else:
                        cmax_next[i] = score_tile(qs_next[i], i, 0, 0, t, cmax_next[i])
                    s = s_ref[i, c % 2, t * KEY_TILE:(t + 1) * KEY_TILE, :]
                    p = jnp.exp2(s - m_new[i]).astype(jnp.bfloat16)
                    if pending[i] is not None:
                        issue_pv(i)
                    pending[i] = (p, c * KEY_CHUNK + t * KEY_TILE, alpha[i], t == n_tiles - 1)
            m, cmax = m_new, cmax_next
        for i in range(n_maps):
            issue_pv(i)
        finish(qi, acc)
        return tuple(cmax)

    q0 = load_q(0)
    cmax0 = [neg_inf((8, tq)) for _ in range(n_maps)]
    for t in range(n_tiles):
        for i in range(n_maps):
            cmax0[i] = score_tile(q0[i], i, 0, 0, t, cmax0[i])
    lax.fori_loop(0, n_q, q_tile, tuple(cmax0))


def _mla_attn_kernel(k_ref, qt_ref, vt_ref, gt_ref, o_ref, s_ref, *, seq, n_q):
    f32 = jnp.float32
    tq = qt_ref.shape[-1]
    per_step = MLA_TILES_PER_STEP

    def load_q(qi):
        return [qt_ref[0, 0, per_step * qi + j] for j in range(per_step)]

    def finish(qi, accs):
        for j, acc in enumerate(accs):
            o = acc[:MLA_V] * (1.0 / acc[MLA_V:MLA_V + 1])
            tile = per_step * qi + j
            o_ref[0, 0, tile] = (o * gt_ref[0, 0, tile].astype(f32)).astype(o_ref.dtype)

    _pipelined_attention(k_ref, vt_ref, s_ref, load_q, finish, seq=seq, n_q=n_q // per_step, tq=tq,
                         n_maps=per_step)


def _diff_attn_kernel(k_ref, qt_ref, vt_ref, gt_ref, lam_ref, gd_ref, o_ref, s_ref, *, seq, n_q, lam_init):
    f32 = jnp.float32
    tq = qt_ref.shape[-1]
    dv = 2 * DIFF_HEAD_DIM
    lp = lam_ref[...].astype(f32)
    lam = (jnp.exp(jnp.sum(lp[0:1] * lp[1:2], axis=1, keepdims=True))
           - jnp.exp(jnp.sum(lp[2:3] * lp[3:4], axis=1, keepdims=True)) + lam_init)
    first = lax.broadcasted_iota(jnp.int32, (LANES, tq), 0) < DIFF_HEAD_DIM
    per_step = DIFF_TILES_PER_STEP

    def load_q(qi):
        maps = []
        for j in range(per_step):
            qt = qt_ref[0, 0, per_step * qi + j]
            zero = jnp.zeros_like(qt)
            maps += [jnp.where(first, qt, zero), jnp.where(first, zero, qt)]
        return maps

    def finish(qi, accs):
        for j in range(per_step):
            a1, a2 = accs[2 * j], accs[2 * j + 1]
            o = a1[:dv] * (1.0 / a1[dv:dv + 1]) - lam * (a2[:dv] * (1.0 / a2[dv:dv + 1]))
            o = o * lax.rsqrt(jnp.mean(o * o, axis=0, keepdims=True) + DIFF_RMS_EPS)
            o = o * gd_ref[...] * (1.0 - lam_init)
            tile = per_step * qi + j
            o_ref[0, 0, tile] = (o * gt_ref[0, 0, tile].astype(f32)).astype(o_ref.dtype)

    _pipelined_attention(k_ref, vt_ref, s_ref, load_q, finish, seq=seq, n_q=n_q // per_step, tq=tq,
                         n_maps=2 * per_step)


def _out_kernel(x_ref, za_ref, zb_ref, gm_ref, wa_ref, wb_ref, wo_ref, lg_ref, lb_ref, o_ref):
    bf16, f32 = jnp.bfloat16, jnp.float32
    tm = TOKEN_TILE
    for j in range(STEP_TILES):
        rows = slice(j * tm, (j + 1) * tm)
        za = za_ref[0, :, j].astype(f32).reshape(MLA_WIDTH, tm).T.astype(bf16)
        zb = zb_ref[0, :, j].astype(f32).reshape(DIFF_WIDTH, tm).T.astype(bf16)
        ya = jnp.dot(za, wa_ref[...], preferred_element_type=f32)
        yb = jnp.dot(zb, wb_ref[...], preferred_element_type=f32)
        gm = gm_ref[0, rows, :].astype(f32)
        merged = gm[:, :D_MODEL] * ya + gm[:, D_MODEL:] * yb
        out = jnp.dot(merged.astype(bf16), wo_ref[...], preferred_element_type=f32)
        r = DEEPNORM_ALPHA * x_ref[0, rows, :] + out
        mu = jnp.mean(r, axis=-1, keepdims=True)
        d = r - mu
        var = jnp.mean(d * d, axis=-1, keepdims=True)
        o_ref[0, rows, :] = d * lax.rsqrt(var + LN_EPS) * lg_ref[...] + lb_ref[...]


def _layer_spec(shape):
    return lambda l: pl.BlockSpec((None,) + tuple(shape), lambda *_: (l,) + (0,) * len(shape))


def _prepare_weights(p):
    bf16 = jnp.bfloat16
    w_in = p["w_in"]
    kr_cols = jnp.pad(w_in[:, :, _IN_KR:_IN_REST], ((0, 0), (0, 0), (MLA_NOPE, LANES - MLA_NOPE - MLA_ROPE)))
    w_lat = jnp.concatenate([w_in[:, :, :_IN_KR], kr_cols], axis=2).astype(bf16)
    w_rest = w_in[:, :, _IN_REST:].astype(bf16)
    wq = p["w_q_up"].reshape(DEPTH, MLA_Q_LORA, MLA_HEADS, MLA_NOPE + MLA_ROPE)
    wq = jnp.pad(wq, ((0, 0), (0, 0), (0, 0), (0, LANES - MLA_NOPE - MLA_ROPE)))
    wq = wq.reshape(DEPTH, MLA_Q_LORA, MLA_HEADS * LANES).astype(bf16)
    wkv = p["w_kv_up"].reshape(DEPTH, MLA_KV_LORA, MLA_HEADS, MLA_NOPE + MLA_V)
    wkn = jnp.pad(wkv[..., :MLA_NOPE], ((0, 0), (0, 0), (0, 0), (0, LANES - MLA_NOPE)))
    wkn = wkn.reshape(DEPTH, MLA_KV_LORA, MLA_HEADS * LANES).astype(bf16)
    wv = wkv[..., MLA_NOPE:].reshape(DEPTH, MLA_KV_LORA, MLA_WIDTH).astype(bf16)
    return dict(
        w_lat=w_lat, w_rest=w_rest, wq=wq, wkn=wkn, wv=wv,
        g_q=p["g_q"][:, None], g_kv=p["g_kv"][:, None], b_merge=p["b_merge"][:, None],
        diff_lambda=p["diff_lambda"], g_diff=p["g_diff"][:, :, None],
        wa=p["w_branch_a"].astype(bf16), wb=p["w_branch_b"].astype(bf16), wo=p["w_out"].astype(bf16),
        ln_gamma=p["ln_gamma"][:, None], ln_beta=p["ln_beta"][:, None])


def _layer(x, w, l, tab_a, tab_b):
    bf16, f32 = jnp.bfloat16, jnp.float32
    batch, seq, _ = x.shape
    tm = TOKEN_TILE
    n_q = seq // tm
    step = STEP_TILES * tm
    cparams = functools.partial(pltpu.CompilerParams, vmem_limit_bytes=VMEM_LIMIT_BYTES)

    va_rows = MLA_V + ONES_ROWS
    vb_rows = 2 * DIFF_HEAD_DIM + ONES_ROWS
    tok = lambda b, i: (b, i, 0)
    tile5 = lambda b, i: (b, 0, i, 0, 0)
    lane4 = lambda b, i: (b, 0, 0, i)
    tab_spec = pl.BlockSpec((3, step, LANES), lambda b, i: (0, i, 0))
    ka, qat, vat, gat, kb, qbt, vbt, gbt, gm = pl.pallas_call(
        _proj_kernel,
        grid=(batch, seq // step),
        in_specs=[
            pl.BlockSpec((1, step, D_MODEL), tok),
            _layer_spec(w["w_lat"].shape[1:])(l), _layer_spec(w["w_rest"].shape[1:])(l),
            _layer_spec(w["wq"].shape[1:])(l), _layer_spec(w["wkn"].shape[1:])(l),
            _layer_spec(w["wv"].shape[1:])(l),
            _layer_spec((1, MLA_Q_LORA))(l), _layer_spec((1, MLA_KV_LORA))(l),
            _layer_spec((1, 2 * D_MODEL))(l),
            tab_spec, tab_spec,
        ],
        out_specs=[
            pl.BlockSpec((1, step, MLA_HEADS * LANES), tok),
            pl.BlockSpec((1, MLA_HEADS, STEP_TILES, LANES, tm), tile5),
            pl.BlockSpec((1, MLA_HEADS, va_rows, step), lane4),
            pl.BlockSpec((1, MLA_HEADS, STEP_TILES, MLA_V, tm), tile5),
            pl.BlockSpec((1, step, DIFF_WIDTH), tok),
            pl.BlockSpec((1, DIFF_HEADS, STEP_TILES, LANES, tm), tile5),
            pl.BlockSpec((1, DIFF_HEADS, vb_rows, step), lane4),
            pl.BlockSpec((1, DIFF_HEADS, STEP_TILES, LANES, tm), tile5),
            pl.BlockSpec((1, step, 2 * D_MODEL), tok),
        ],
        out_shape=[
            jax.ShapeDtypeStruct((batch, seq, MLA_HEADS * LANES), bf16),
            jax.ShapeDtypeStruct((batch, MLA_HEADS, n_q, LANES, tm), bf16),
            jax.ShapeDtypeStruct((batch, MLA_HEADS, va_rows, seq), bf16),
            jax.ShapeDtypeStruct((batch, MLA_HEADS, n_q, MLA_V, tm), bf16),
            jax.ShapeDtypeStruct((batch, seq, DIFF_WIDTH), bf16),
            jax.ShapeDtypeStruct((batch, DIFF_HEADS, n_q, LANES, tm), bf16),
            jax.ShapeDtypeStruct((batch, DIFF_HEADS, vb_rows, seq), bf16),
            jax.ShapeDtypeStruct((batch, DIFF_HEADS, n_q, LANES, tm), bf16),
            jax.ShapeDtypeStruct((batch, seq, 2 * D_MODEL), bf16),
        ],
        compiler_params=cparams(dimension_semantics=("arbitrary", "arbitrary")),
        name="proj",
    )(x, w["w_lat"], w["w_rest"], w["wq"], w["wkn"], w["wv"], w["g_q"], w["g_kv"], w["b_merge"],
      tab_a, tab_b)

    head4 = lambda b, h: (b, h, 0, 0)
    head5 = lambda b, h: (b, h, 0, 0, 0)
    za = pl.pallas_call(
        functools.partial(_mla_attn_kernel, seq=seq, n_q=n_q),
        grid=(batch, MLA_HEADS),
        in_specs=[
            pl.BlockSpec((1, seq, LANES), lambda b, h: (b, 0, h)),
            pl.BlockSpec((1, 1, n_q, LANES, tm), head5),
            pl.BlockSpec((1, 1, va_rows, seq), head4),
            pl.BlockSpec((1, 1, n_q, MLA_V, tm), head5),
        ],
        out_specs=pl.BlockSpec((1, 1, n_q, MLA_V, tm), head5),
        out_shape=jax.ShapeDtypeStruct((batch, MLA_HEADS, n_q, MLA_V, tm), bf16),
        scratch_shapes=[pltpu.VMEM((MLA_TILES_PER_STEP, 2, KEY_CHUNK, tm), f32)],
        compiler_params=cparams(dimension_semantics=("arbitrary", "arbitrary")),
        name="mla_attn",
    )(ka, qat, vat, gat)

    lam_init = 0.8 - 0.6 * math.exp(-0.3 * l)
    zb = pl.pallas_call(
        functools.partial(_diff_attn_kernel, seq=seq, n_q=n_q, lam_init=lam_init),
        grid=(batch, DIFF_HEADS),
        in_specs=[
            pl.BlockSpec((1, seq, LANES), lambda b, h: (b, 0, h)),
            pl.BlockSpec((1, 1, n_q, LANES, tm), head5),
            pl.BlockSpec((1, 1, vb_rows, seq), head4),
            pl.BlockSpec((1, 1, n_q, LANES, tm), head5),
            _layer_spec((4, DIFF_HEAD_DIM))(l),
            _layer_spec((2 * DIFF_HEAD_DIM, 1))(l),
        ],
        out_specs=pl.BlockSpec((1, 1, n_q, LANES, tm), head5),
        out_shape=jax.ShapeDtypeStruct((batch, DIFF_HEADS, n_q, LANES, tm), bf16),
        scratch_shapes=[pltpu.VMEM((2 * DIFF_TILES_PER_STEP, 2, KEY_CHUNK, tm), f32)],
        compiler_params=cparams(dimension_semantics=("arbitrary", "arbitrary")),
        name="diff_attn",
    )(kb, qbt, vbt, gbt, w["diff_lambda"], w["g_diff"])

    return pl.pallas_call(
        _out_kernel,
        grid=(batch, seq // step),
        in_specs=[
            pl.BlockSpec((1, step, D_MODEL), tok),
            pl.BlockSpec((1, MLA_HEADS, STEP_TILES, MLA_V, tm), tile5),
            pl.BlockSpec((1, DIFF_HEADS, STEP_TILES, LANES, tm), tile5),
            pl.BlockSpec((1, step, 2 * D_MODEL), tok),
            _layer_spec((MLA_WIDTH, D_MODEL))(l), _layer_spec((DIFF_WIDTH, D_MODEL))(l),
            _layer_spec((D_MODEL, D_MODEL))(l),
            _layer_spec((1, D_MODEL))(l), _layer_spec((1, D_MODEL))(l),
        ],
        out_specs=pl.BlockSpec((1, step, D_MODEL), tok),
        out_shape=jax.ShapeDtypeStruct((batch, seq, D_MODEL), f32),
        compiler_params=cparams(dimension_semantics=("arbitrary", "arbitrary")),
        name="out_proj",
    )(x, za, zb, gm, w["wa"], w["wb"], w["wo"], w["ln_gamma"], w["ln_beta"])


def kernel(x, w_in, g_q, w_q_up, g_kv, w_kv_up, diff_lambda, g_diff, w_branch_a, w_branch_b,
           b_merge, w_out, ln_gamma, ln_beta):
    weights = _prepare_weights(dict(
        w_in=w_in, g_q=g_q, w_q_up=w_q_up, g_kv=g_kv, w_kv_up=w_kv_up, diff_lambda=diff_lambda,
        g_diff=g_diff, w_branch_a=w_branch_a, w_branch_b=w_branch_b, b_merge=b_merge, w_out=w_out,
        ln_gamma=ln_gamma, ln_beta=ln_beta))
    seq = x.shape[1]
    tab_a = _rotary_tables(seq, MLA_ROPE, MLA_NOPE, LANES)
    tab_b = _rotary_tables(seq, DIFF_ROT, 0, DIFF_HEAD_DIM)
    for l in range(DEPTH):
        x = _layer(x, weights, l, tab_a, tab_b)
    return x
```

```python
import functools
import math

import jax
import jax.numpy as jnp
from jax import lax
from jax.experimental import pallas as pl
from jax.experimental.pallas import tpu as pltpu

D_MODEL = 1024
DEPTH = 2
MLA_HEADS = 8
MLA_Q_LORA = 256
MLA_KV_LORA = 128
MLA_NOPE = 64
MLA_ROPE = 32
MLA_V = 64
MLA_WIDTH = MLA_HEADS * MLA_V
DIFF_HEADS = 4
DIFF_HEAD_DIM = 64
DIFF_WIDTH = DIFF_HEADS * 2 * DIFF_HEAD_DIM
DIFF_ROT = DIFF_HEAD_DIM // 4
ROPE_THETA = 500000.0
DEEPNORM_ALPHA = (2 * DEPTH) ** 0.25
LN_EPS = 1e-5
RMS_EPS = 1e-6
DIFF_RMS_EPS = 1e-5
LOG2_E = math.log2(math.e)

LANES = 128
TOKEN_TILE = 256
STEP_TILES = 2
KEY_TILE = 256
MLA_KEY_CHUNK = 256
DIFF_KEY_CHUNK = 256
ONES_ROWS = 16
MLA_TILES_PER_STEP = 4
DIFF_TILES_PER_STEP = 2
VMEM_LIMIT_BYTES = 56 * 1024 * 1024

_IN_KR = MLA_Q_LORA + MLA_KV_LORA
_IN_REST = _IN_KR + MLA_ROPE
_L_KV = MLA_Q_LORA
_L_KR = _L_KV + MLA_KV_LORA
_L_END = _L_KR + LANES
_R_GA = 0
_R_QD = _R_GA + MLA_WIDTH
_R_KD = _R_QD + DIFF_WIDTH
_R_VD = _R_KD + DIFF_WIDTH
_R_GB = _R_VD + DIFF_WIDTH
_R_GM = _R_GB + DIFF_WIDTH
_R_END = _R_GM + 2 * D_MODEL


def _rotary_tables(seq, rot_dim, first_lane, period):
    half = rot_dim // 2
    inv_freq = ROPE_THETA ** (-jnp.arange(half, dtype=jnp.float32) / half)
    ang = jnp.arange(seq, dtype=jnp.float32)[:, None] * inv_freq[None, :]
    cos, sin = jnp.cos(ang), jnp.sin(ang)
    lane = jnp.arange(LANES) % period - first_lane
    lo = (lane >= 0) & (lane < half)
    hi = (lane >= half) & (lane < rot_dim)
    idx = jnp.clip(jnp.where(hi, lane - half, lane), 0, half - 1)
    cos_l, sin_l = cos[:, idx], sin[:, idx]
    c = jnp.where((lo | hi)[None, :], cos_l, 1.0)
    s_lo = jnp.where(lo[None, :], -sin_l, 0.0)
    s_hi = jnp.where(hi[None, :], sin_l, 0.0)
    return jnp.stack([c, s_lo, s_hi]).astype(jnp.float32)


def _rotate(x, tab, half):
    up = pltpu.roll(x, LANES - half, 1)
    down = pltpu.roll(x, half, 1)
    return x * tab[0] + up * tab[1] + down * tab[2]


def _rms_norm(x, g, eps):
    return x * lax.rsqrt(jnp.mean(x * x, axis=-1, keepdims=True) + eps) * g


def _with_ones_rows(vt):
    heads, _, tokens = vt.shape
    return jnp.concatenate([vt, jnp.ones((heads, ONES_ROWS, tokens), vt.dtype)], axis=1)


def _proj_kernel(x_ref, wlat_ref, wrest_ref, wq_ref, wkn_ref, wv_ref, gq_ref, gkv_ref, bm_ref,
                 taba_ref, tabb_ref,
                 ka_ref, qat_ref, vat_ref, gat_ref, kb_ref, qbt_ref, vbt_ref, gbt_ref, gm_ref):
    bf16, f32 = jnp.bfloat16, jnp.float32
    tm = TOKEN_TILE
    mla_scale = LOG2_E / math.sqrt(MLA_NOPE + MLA_ROPE)
    diff_scale = LOG2_E / math.sqrt(DIFF_HEAD_DIM)

    for j in range(STEP_TILES):
        rows = slice(j * tm, (j + 1) * tm)
        xb = x_ref[0, rows, :].astype(bf16)
        tab_a = [taba_ref[i, rows, :] for i in range(3)]
        tab_b = [tabb_ref[i, rows, :] for i in range(3)]

        def proj(lo, hi):
            return jnp.dot(xb, wrest_ref[:, lo:hi], preferred_element_type=f32)

        lat = jnp.dot(xb, wlat_ref[...], preferred_element_type=f32)
        cq = _rms_norm(lat[:, :_L_KV], gq_ref[...], RMS_EPS).astype(bf16)
        ckv = _rms_norm(lat[:, _L_KV:_L_KR], gkv_ref[...], RMS_EPS).astype(bf16)
        k_rope = _rotate(lat[:, _L_KR:_L_END], tab_a, MLA_ROPE // 2)
        q_all = jnp.dot(cq, wq_ref[...], preferred_element_type=f32)
        k_all = jnp.dot(ckv, wkn_ref[...], preferred_element_type=f32)
        for h in range(MLA_HEADS):
            sl = slice(h * LANES, (h + 1) * LANES)
            q_h = _rotate(q_all[:, sl], tab_a, MLA_ROPE // 2) * mla_scale
            qat_ref[0, h, j] = q_h.T.astype(bf16)
            ka_ref[0, rows, sl] = (k_all[:, sl] + k_rope).astype(bf16)
        v = jnp.dot(ckv, wv_ref[...], preferred_element_type=f32)
        vat_ref[0, :, :, rows] = _with_ones_rows(v.T.reshape(MLA_HEADS, MLA_V, tm)).astype(bf16)
        ga = proj(_R_GA, _R_QD)
        gat_ref[0, :, j] = (ga * jax.nn.sigmoid(ga)).T.reshape(MLA_HEADS, MLA_V, tm).astype(bf16)

        qk_d = proj(_R_QD, _R_VD)
        for h in range(DIFF_HEADS):
            sl = slice(h * LANES, (h + 1) * LANES)
            q_h = _rotate(qk_d[:, sl], tab_b, DIFF_ROT // 2)
            qbt_ref[0, h, j] = (q_h * diff_scale).T.astype(bf16)
            k_h = _rotate(qk_d[:, DIFF_WIDTH + h * LANES:DIFF_WIDTH + (h + 1) * LANES], tab_b, DIFF_ROT // 2)
            kb_ref[0, rows, sl] = k_h.astype(bf16)
        vbt_ref[0, :, :, rows] = _with_ones_rows(proj(_R_VD, _R_GB).T.reshape(DIFF_HEADS, LANES, tm)).astype(bf16)
        gb = proj(_R_GB, _R_GM)
        gbt_ref[0, :, j] = (gb * jax.nn.sigmoid(gb)).T.reshape(DIFF_HEADS, LANES, tm).astype(bf16)

        gm_ref[0, rows, :] = jax.nn.sigmoid(proj(_R_GM, _R_END) + bm_ref[...]).astype(bf16)


def _pipelined_attention(k_ref, vt_ref, s_ref, load_q, finish, *, seq, n_q, tq, n_maps, key_chunk, pv_first):
    f32 = jnp.float32
    n_chunks = seq // key_chunk
    n_tiles = key_chunk // KEY_TILE
    assert n_chunks % 2 == 0
    rows = vt_ref.shape[2]
    neg_inf = lambda shape: jnp.full(shape, -jnp.inf, f32)
    by_sublane = lambda x: x.reshape(KEY_TILE // 8, 8, tq)

    def score_tile(q, i, chunk, slot, t, cmax):
        lo = chunk * key_chunk + t * KEY_TILE
        s = jnp.dot(k_ref[0, lo:lo + KEY_TILE, :], q, preferred_element_type=f32)
        s_ref[i, slot, t * KEY_TILE:(t + 1) * KEY_TILE, :] = s
        return jnp.maximum(cmax, jnp.max(by_sublane(s), axis=0))

    def q_tile(qi, cmax):
        qs = load_q(qi)
        qs_next = load_q(jnp.minimum(qi + 1, n_q - 1))
        cmax = list(cmax)
        m = [neg_inf((1, tq)) for _ in range(n_maps)]
        acc = [jnp.zeros((rows, tq), f32) for _ in range(n_maps)]
        pending = [None] * n_maps
        pv = [None] * n_maps

        def issue_pv(i):
            p, lo, alpha_c, last = pending[i]
            d = jnp.dot(vt_ref[0, 0, :, lo:lo + KEY_TILE], p, preferred_element_type=f32)
            pv[i] = d if pv[i] is None else pv[i] + d
            if last:
                acc[i] = alpha_c * acc[i] + pv[i]
                pv[i] = None
            pending[i] = None

        for c in range(n_chunks):
            m_new = [jnp.maximum(m[i], jnp.max(cmax[i], axis=0, keepdims=True)) for i in range(n_maps)]
            alpha = [jnp.exp2(m[i] - m_new[i]) for i in range(n_maps)]
            cmax_next = [neg_inf((8, tq)) for _ in range(n_maps)]
            for t in range(n_tiles):
                for i in range(n_maps):
                    if pv_first and pending[i] is not None:
                        issue_pv(i)
                    if c + 1 < n_chunks:
                        cmax_next[i] = score_tile(qs[i], i, c + 1, (c + 1) % 2, t, cmax_next[i])
                    else:
                        cmax_next[i] = score_tile(qs_next[i], i, 0, 0, t, cmax_next[i])
                    s = s_ref[i, c % 2, t * KEY_TILE:(t + 1) * KEY_TILE, :]
                    p = jnp.exp2(s - m_new[i]).astype(jnp.bfloat16)
                    if pending[i] is not None:
                        issue_pv(i)
                    pending[i] = (p, c * key_chunk + t * KEY_TILE, alpha[i], t == n_tiles - 1)
            m, cmax = m_new, cmax_next
        for i in range(n_maps):
            issue_pv(i)
        finish(qi, acc)
        return tuple(cmax)

    q0 = load_q(0)
    cmax0 = [neg_inf((8, tq)) for _ in range(n_maps)]
    for t in range(n_tiles):
        for i in range(n_maps):
            cmax0[i] = score_tile(q0[i], i, 0, 0, t, cmax0[i])
    lax.fori_loop(0, n_q, q_tile, tuple(cmax0))


def _mla_attn_kernel(k_ref, qt_ref, vt_ref, gt_ref, o_ref, s_ref, *, seq, n_q):
    f32 = jnp.float32
    tq = qt_ref.shape[-1]
    per_step = MLA_TILES_PER_STEP

    def load_q(qi):
        return [qt_ref[0, 0, per_step * qi + j] for j in range(per_step)]

    def finish(qi, accs):
        for j, acc in enumerate(accs):
            o = acc[:MLA_V] * (1.0 / acc[MLA_V:MLA_V + 1])
            tile = per_step * qi + j
            o_ref[0, 0, tile] = (o * gt_ref[0, 0, tile].astype(f32)).astype(o_ref.dtype)

    _pipelined_attention(k_ref, vt_ref, s_ref, load_q, finish, seq=seq, n_q=n_q // per_step, tq=tq,
                         n_maps=per_step, key_chunk=MLA_KEY_CHUNK, pv_first=True)


def _diff_attn_kernel(k_ref, qt_ref, vt_ref, gt_ref, lam_ref, gd_ref, o_ref, s_ref, *,
                      seq, n_q, lam_init):
    f32 = jnp.float32
    tq = qt_ref.shape[-1]
    lp = lam_ref[...].astype(f32)
    lam = (jnp.exp(jnp.sum(lp[0:1] * lp[1:2], axis=1, keepdims=True))
           - jnp.exp(jnp.sum(lp[2:3] * lp[3:4], axis=1, keepdims=True)) + lam_init)
    first = lax.broadcasted_iota(jnp.int32, (LANES, tq), 0) < DIFF_HEAD_DIM
    per_step = DIFF_TILES_PER_STEP

    def load_q(qi):
        maps = []
        for j in range(per_step):
            qt = qt_ref[0, 0, per_step * qi + j]
            zero = jnp.zeros_like(qt)
            maps += [jnp.where(first, qt, zero), jnp.where(first, zero, qt)]
        return maps

    def finish(qi, accs):
        dv = 2 * DIFF_HEAD_DIM
        for j in range(per_step):
            a1, a2 = accs[2 * j], accs[2 * j + 1]
            o = a1[:dv] * (1.0 / a1[dv:dv + 1]) - lam * (a2[:dv] * (1.0 / a2[dv:dv + 1]))
            o = o * lax.rsqrt(jnp.mean(o * o, axis=0, keepdims=True) + DIFF_RMS_EPS)
            o = o * gd_ref[...] * (1.0 - lam_init)
            tile = per_step * qi + j
            o_ref[0, 0, tile] = (o * gt_ref[0, 0, tile].astype(f32)).astype(o_ref.dtype)

    _pipelined_attention(k_ref, vt_ref, s_ref, load_q, finish, seq=seq, n_q=n_q // per_step, tq=tq,
                         n_maps=2 * per_step, key_chunk=DIFF_KEY_CHUNK, pv_first=True)


def _out_kernel(x_ref, za_ref, zb_ref, gm_ref, wa_ref, wb_ref, wo_ref, lg_ref, lb_ref, o_ref):
    bf16, f32 = jnp.bfloat16, jnp.float32
    tm = TOKEN_TILE
    for j in range(STEP_TILES):
        rows = slice(j * tm, (j + 1) * tm)
        za = za_ref[0, :, j].astype(f32).reshape(MLA_WIDTH, tm).T.astype(bf16)
        zb = zb_ref[0, :, j].astype(f32).reshape(DIFF_WIDTH, tm).T.astype(bf16)
        ya = jnp.dot(za, wa_ref[...], preferred_element_type=f32)
        yb = jnp.dot(zb, wb_ref[...], preferred_element_type=f32)
        gm = gm_ref[0, rows, :].astype(f32)
        merged = gm[:, :D_MODEL] * ya + gm[:, D_MODEL:] * yb
        out = jnp.dot(merged.astype(bf16), wo_ref[...], preferred_element_type=f32)
        r = DEEPNORM_ALPHA * x_ref[0, rows, :] + out
        mu = jnp.mean(r, axis=-1, keepdims=True)
        d = r - mu
        var = jnp.mean(d * d, axis=-1, keepdims=True)
        o_ref[0, rows, :] = d * lax.rsqrt(var + LN_EPS) * lg_ref[...] + lb_ref[...]


def _layer_spec(shape):
    return lambda l: pl.BlockSpec((None,) + tuple(shape), lambda *_: (l,) + (0,) * len(shape))


def _prepare_weights(p):
    bf16 = jnp.bfloat16
    w_in = p["w_in"].astype(bf16)
    kr_cols = jnp.pad(w_in[:, :, _IN_KR:_IN_REST], ((0, 0), (0, 0), (MLA_NOPE, LANES - MLA_NOPE - MLA_ROPE)))
    w_lat = jnp.concatenate([w_in[:, :, :_IN_KR], kr_cols], axis=2)
    w_rest = w_in[:, :, _IN_REST:]
    wq = p["w_q_up"].reshape(DEPTH, MLA_Q_LORA, MLA_HEADS, MLA_NOPE + MLA_ROPE)
    wq = jnp.pad(wq, ((0, 0), (0, 0), (0, 0), (0, LANES - MLA_NOPE - MLA_ROPE)))
    wq = wq.reshape(DEPTH, MLA_Q_LORA, MLA_HEADS * LANES).astype(bf16)
    wkv = p["w_kv_up"].reshape(DEPTH, MLA_KV_LORA, MLA_HEADS, MLA_NOPE + MLA_V)
    wkn = jnp.pad(wkv[..., :MLA_NOPE], ((0, 0), (0, 0), (0, 0), (0, LANES - MLA_NOPE)))
    wkn = wkn.reshape(DEPTH, MLA_KV_LORA, MLA_HEADS * LANES).astype(bf16)
    wv = wkv[..., MLA_NOPE:].reshape(DEPTH, MLA_KV_LORA, MLA_WIDTH).astype(bf16)
    return dict(
        w_lat=w_lat, w_rest=w_rest, wq=wq, wkn=wkn, wv=wv,
        g_q=p["g_q"][:, None], g_kv=p["g_kv"][:, None], b_merge=p["b_merge"][:, None],
        diff_lambda=p["diff_lambda"], g_diff=p["g_diff"][:, :, None],
        wa=p["w_branch_a"].astype(bf16), wb=p["w_branch_b"].astype(bf16), wo=p["w_out"].astype(bf16),
        ln_gamma=p["ln_gamma"][:, None], ln_beta=p["ln_beta"][:, None])


def _layer(x, w, l, tab_a, tab_b):
    bf16, f32 = jnp.bfloat16, jnp.float32
    batch, seq, _ = x.shape
    tm = TOKEN_TILE
    n_q = seq // tm
    step = STEP_TILES * tm
    cparams = functools.partial(pltpu.CompilerParams, vmem_limit_bytes=VMEM_LIMIT_BYTES)

    va_rows = MLA_V + ONES_ROWS
    vb_rows = 2 * DIFF_HEAD_DIM + ONES_ROWS
    tok = lambda b, i: (b, i, 0)
    tile5 = lambda b, i: (b, 0, i, 0, 0)
    lane4 = lambda b, i: (b, 0, 0, i)
    tab_spec = pl.BlockSpec((3, step, LANES), lambda b, i: (0, i, 0))
    ka, qat, vat, gat, kb, qbt, vbt, gbt, gm = pl.pallas_call(
        _proj_kernel,
        grid=(batch, seq // step),
        in_specs=[
            pl.BlockSpec((1, step, D_MODEL), tok),
            _layer_spec(w["w_lat"].shape[1:])(l), _layer_spec(w["w_rest"].shape[1:])(l),
            _layer_spec(w["wq"].shape[1:])(l), _layer_spec(w["wkn"].shape[1:])(l),
            _layer_spec(w["wv"].shape[1:])(l),
            _layer_spec((1, MLA_Q_LORA))(l), _layer_spec((1, MLA_KV_LORA))(l),
            _layer_spec((1, 2 * D_MODEL))(l),
            tab_spec, tab_spec,
        ],
        out_specs=[
            pl.BlockSpec((1, step, MLA_HEADS * LANES), tok),
            pl.BlockSpec((1, MLA_HEADS, STEP_TILES, LANES, tm), tile5),
            pl.BlockSpec((1, MLA_HEADS, va_rows, step), lane4),
            pl.BlockSpec((1, MLA_HEADS, STEP_TILES, MLA_V, tm), tile5),
            pl.BlockSpec((1, step, DIFF_WIDTH), tok),
            pl.BlockSpec((1, DIFF_HEADS, STEP_TILES, LANES, tm), tile5),
            pl.BlockSpec((1, DIFF_HEADS, vb_rows, step), lane4),
            pl.BlockSpec((1, DIFF_HEADS, STEP_TILES, LANES, tm), tile5),
            pl.BlockSpec((1, step, 2 * D_MODEL), tok),
        ],
        out_shape=[
            jax.ShapeDtypeStruct((batch, seq, MLA_HEADS * LANES), bf16),
            jax.ShapeDtypeStruct((batch, MLA_HEADS, n_q, LANES, tm), bf16),
            jax.ShapeDtypeStruct((batch, MLA_HEADS, va_rows, seq), bf16),
            jax.ShapeDtypeStruct((batch, MLA_HEADS, n_q, MLA_V, tm), bf16),
            jax.ShapeDtypeStruct((batch, seq, DIFF_WIDTH), bf16),
            jax.ShapeDtypeStruct((batch, DIFF_HEADS, n_q, LANES, tm), bf16),
            jax.ShapeDtypeStruct((batch, DIFF_HEADS, vb_rows, seq), bf16),
            jax.ShapeDtypeStruct((batch, DIFF_HEADS, n_q, LANES, tm), bf16),
            jax.ShapeDtypeStruct((batch, seq, 2 * D_MODEL), bf16),
        ],
        compiler_params=cparams(dimension_semantics=("arbitrary", "arbitrary")),
        name="proj",
    )(x, w["w_lat"], w["w_rest"], w["wq"], w["wkn"], w["wv"], w["g_q"], w["g_kv"], w["b_merge"],
      tab_a, tab_b)

    head4 = lambda b, h: (b, h, 0, 0)
    head5 = lambda b, h: (b, h, 0, 0, 0)
    za = pl.pallas_call(
        functools.partial(_mla_attn_kernel, seq=seq, n_q=n_q),
        grid=(batch, MLA_HEADS),
        in_specs=[
            pl.BlockSpec((1, seq, LANES), lambda b, h: (b, 0, h)),
            pl.BlockSpec((1, 1, n_q, LANES, tm), head5),
            pl.BlockSpec((1, 1, va_rows, seq), head4),
            pl.BlockSpec((1, 1, n_q, MLA_V, tm), head5),
        ],
        out_specs=pl.BlockSpec((1, 1, n_q, MLA_V, tm), head5),
        out_shape=jax.ShapeDtypeStruct((batch, MLA_HEADS, n_q, MLA_V, tm), bf16),
        scratch_shapes=[pltpu.VMEM((MLA_TILES_PER_STEP, 2, MLA_KEY_CHUNK, tm), f32)],
        compiler_params=cparams(dimension_semantics=("arbitrary", "arbitrary")),
        name="mla_attn",
    )(ka, qat, vat, gat)

    lam_init = 0.8 - 0.6 * math.exp(-0.3 * l)
    zb = pl.pallas_call(
        functools.partial(_diff_attn_kernel, seq=seq, n_q=n_q, lam_init=lam_init),
        grid=(batch, DIFF_HEADS),
        in_specs=[
            pl.BlockSpec((1, seq, LANES), lambda b, h: (b, 0, h)),
            pl.BlockSpec((1, 1, n_q, LANES, tm), head5),
            pl.BlockSpec((1, 1, vb_rows, seq), head4),
            pl.BlockSpec((1, 1, n_q, LANES, tm), head5),
            _layer_spec((4, DIFF_HEAD_DIM))(l),
            _layer_spec((2 * DIFF_HEAD_DIM, 1))(l),
        ],
        out_specs=pl.BlockSpec((1, 1, n_q, LANES, tm), head5),
        out_shape=jax.ShapeDtypeStruct((batch, DIFF_HEADS, n_q, LANES, tm), bf16),
        scratch_shapes=[pltpu.VMEM((2 * DIFF_TILES_PER_STEP, 2, DIFF_KEY_CHUNK, tm), f32)],
        compiler_params=cparams(dimension_semantics=("arbitrary", "arbitrary")),
        name="diff_attn",
    )(kb, qbt, vbt, gbt, w["diff_lambda"], w["g_diff"])

    return pl.pallas_call(
        _out_kernel,
        grid=(batch, seq // step),
        in_specs=[
            pl.BlockSpec((1, step, D_MODEL), tok),
            pl.BlockSpec((1, MLA_HEADS, STEP_TILES, MLA_V, tm), tile5),
            pl.BlockSpec((1, DIFF_HEADS, STEP_TILES, LANES, tm), tile5),
            pl.BlockSpec((1, step, 2 * D_MODEL), tok),
            _layer_spec((MLA_WIDTH, D_MODEL))(l), _layer_spec((DIFF_WIDTH, D_MODEL))(l),
            _layer_spec((D_MODEL, D_MODEL))(l),
            _layer_spec((1, D_MODEL))(l), _layer_spec((1, D_MODEL))(l),
        ],
        out_specs=pl.BlockSpec((1, step, D_MODEL), tok),
        out_shape=jax.ShapeDtypeStruct((batch, seq, D_MODEL), f32),
        compiler_params=cparams(dimension_semantics=("arbitrary", "arbitrary")),
        name="out_proj",
    )(x, za, zb, gm, w["wa"], w["wb"], w["wo"], w["ln_gamma"], w["ln_beta"])


def kernel(x, w_in, g_q, w_q_up, g_kv, w_kv_up, diff_lambda, g_diff, w_branch_a, w_branch_b,
           b_merge, w_out, ln_gamma, ln_beta):
    weights = _prepare_weights(dict(
        w_in=w_in, g_q=g_q, w_q_up=w_q_up, g_kv=g_kv, w_kv_up=w_kv_up, diff_lambda=diff_lambda,
        g_diff=g_diff, w_branch_a=w_branch_a, w_branch_b=w_branch_b, b_merge=b_merge, w_out=w_out,
        ln_gamma=ln_gamma, ln_beta=ln_beta))
    seq = x.shape[1]
    tab_a = _rotary_tables(seq, MLA_ROPE, MLA_NOPE, LANES)
    tab_b = _rotary_tables(seq, DIFF_ROT, 0, DIFF_HEAD_DIM)
    for l in range(DEPTH):
        x = _layer(x, weights, l, tab_a, tab_b)
    return x
```

```python
import functools
import math

import jax
import jax.numpy as jnp
from jax import lax
from jax.experimental import pallas as pl
from jax.experimental.pallas import tpu as pltpu

D_MODEL = 1024
DEPTH = 2
MLA_HEADS = 8
MLA_Q_LORA = 256
MLA_KV_LORA = 128
MLA_NOPE = 64
MLA_ROPE = 32
MLA_V = 64
MLA_WIDTH = MLA_HEADS * MLA_V
DIFF_HEADS = 4
DIFF_HEAD_DIM = 64
DIFF_WIDTH = DIFF_HEADS * 2 * DIFF_HEAD_DIM
DIFF_ROT = DIFF_HEAD_DIM // 4
ROPE_THETA = 500000.0
DEEPNORM_ALPHA = (2 * DEPTH) ** 0.25
LN_EPS = 1e-5
RMS_EPS = 1e-6
DIFF_RMS_EPS = 1e-5
LOG2_E = math.log2(math.e)

LANES = 128
TOKEN_TILE = 256
STEP_TILES = 2
OUT_STEP_TILES = 4
WEIGHT_ROWS = 256
KEY_TILE = 256
MLA_KEY_CHUNK = 256
DIFF_KEY_CHUNK = 512
ONES_ROWS = 16
MLA_TILES_PER_STEP = 4
DIFF_TILES_PER_STEP = 2
VMEM_LIMIT_BYTES = 56 * 1024 * 1024

_IN_KR = MLA_Q_LORA + MLA_KV_LORA
_IN_REST = _IN_KR + MLA_ROPE
_L_KV = MLA_Q_LORA
_L_KR = _L_KV + MLA_KV_LORA
_L_END = _L_KR + LANES
_R_GA = 0
_R_QD = _R_GA + MLA_WIDTH
_R_KD = _R_QD + DIFF_WIDTH
_R_VD = _R_KD + DIFF_WIDTH
_R_GB = _R_VD + DIFF_WIDTH
_R_GM = _R_GB + DIFF_WIDTH
_R_END = _R_GM + 2 * D_MODEL


def _rotary_tables(seq, rot_dim, first_lane, period):
    half = rot_dim // 2
    inv_freq = ROPE_THETA ** (-jnp.arange(half, dtype=jnp.float32) / half)
    ang = jnp.arange(seq, dtype=jnp.float32)[:, None] * inv_freq[None, :]
    cos, sin = jnp.cos(ang), jnp.sin(ang)
    lane = jnp.arange(LANES) % period - first_lane
    lo = (lane >= 0) & (lane < half)
    hi = (lane >= half) & (lane < rot_dim)
    idx = jnp.clip(jnp.where(hi, lane - half, lane), 0, half - 1)
    cos_l, sin_l = cos[:, idx], sin[:, idx]
    c = jnp.where((lo | hi)[None, :], cos_l, 1.0)
    s_lo = jnp.where(lo[None, :], -sin_l, 0.0)
    s_hi = jnp.where(hi[None, :], sin_l, 0.0)
    return jnp.stack([c, s_lo, s_hi]).astype(jnp.float32)


def _rotate(x, tab, half):
    up = pltpu.roll(x, LANES - half, 1)
    down = pltpu.roll(x, half, 1)
    return x * tab[0] + up * tab[1] + down * tab[2]


def _rms_norm(x, g, eps):
    return x * lax.rsqrt(jnp.mean(x * x, axis=-1, keepdims=True) + eps) * g


def _with_ones_rows(vt):
    heads, _, tokens = vt.shape
    return jnp.concatenate([vt, jnp.ones((heads, ONES_ROWS, tokens), vt.dtype)], axis=1)


def _proj_kernel(x_ref, wlat_ref, wrest_ref, wq_ref, wkn_ref, wv_ref, gq_ref, gkv_ref, bm_ref,
                 taba_ref, tabb_ref,
                 ka_ref, qat_ref, vat_ref, gat_ref, kb_ref, qbt_ref, vbt_ref, gbt_ref, gm_ref):
    bf16, f32 = jnp.bfloat16, jnp.float32
    tm = TOKEN_TILE
    mla_scale = LOG2_E / math.sqrt(MLA_NOPE + MLA_ROPE)
    diff_scale = LOG2_E / math.sqrt(DIFF_HEAD_DIM)

    for j in range(STEP_TILES):
        rows = slice(j * tm, (j + 1) * tm)
        xb = x_ref[0, rows, :].astype(bf16)
        tab_a = [taba_ref[i, rows, :] for i in range(3)]
        tab_b = [tabb_ref[i, rows, :] for i in range(3)]

        def proj(lo, hi):
            return jnp.dot(xb, wrest_ref[:, lo:hi], preferred_element_type=f32)

        lat = jnp.dot(xb, wlat_ref[...], preferred_element_type=f32)
        cq = _rms_norm(lat[:, :_L_KV], gq_ref[...], RMS_EPS).astype(bf16)
        ckv = _rms_norm(lat[:, _L_KV:_L_KR], gkv_ref[...], RMS_EPS).astype(bf16)
        k_rope = _rotate(lat[:, _L_KR:_L_END], tab_a, MLA_ROPE // 2)
        q_all = jnp.dot(cq, wq_ref[...], preferred_element_type=f32)
        k_all = jnp.dot(ckv, wkn_ref[...], preferred_element_type=f32)
        for h in range(MLA_HEADS):
            sl = slice(h * LANES, (h + 1) * LANES)
            q_h = _rotate(q_all[:, sl], tab_a, MLA_ROPE // 2) * mla_scale
            qat_ref[0, h, j] = q_h.T.astype(bf16)
            ka_ref[0, rows, sl] = (k_all[:, sl] + k_rope).astype(bf16)
        v = jnp.dot(ckv, wv_ref[...], preferred_element_type=f32)
        vat_ref[0, :, :, rows] = _with_ones_rows(v.T.reshape(MLA_HEADS, MLA_V, tm)).astype(bf16)
        ga = proj(_R_GA, _R_QD)
        gat_ref[0, :, j] = (ga * jax.nn.sigmoid(ga)).T.reshape(MLA_HEADS, MLA_V, tm).astype(bf16)

        qk_d = proj(_R_QD, _R_VD)
        for h in range(DIFF_HEADS):
            sl = slice(h * LANES, (h + 1) * LANES)
            q_h = _rotate(qk_d[:, sl], tab_b, DIFF_ROT // 2)
            qbt_ref[0, h, j] = (q_h * diff_scale).T.astype(bf16)
            k_h = _rotate(qk_d[:, DIFF_WIDTH + h * LANES:DIFF_WIDTH + (h + 1) * LANES], tab_b, DIFF_ROT // 2)
            kb_ref[0, rows, sl] = k_h.astype(bf16)
        vbt_ref[0, :, :, rows] = _with_ones_rows(proj(_R_VD, _R_GB).T.reshape(DIFF_HEADS, LANES, tm)).astype(bf16)
        gb = proj(_R_GB, _R_GM)
        gbt_ref[0, :, j] = (gb * jax.nn.sigmoid(gb)).T.reshape(DIFF_HEADS, LANES, tm).astype(bf16)

        gm_ref[0, rows, :] = jax.nn.sigmoid(proj(_R_GM, _R_END) + bm_ref[...]).astype(bf16)


def _pipelined_attention(k_ref, vt_ref, s_ref, load_q, finish, *, seq, n_q, tq, n_maps, key_chunk, pv_first):
    f32 = jnp.float32
    n_chunks = seq // key_chunk
    n_tiles = key_chunk // KEY_TILE
    assert n_chunks % 2 == 0
    rows = vt_ref.shape[2]
    neg_inf = lambda shape: jnp.full(shape, -jnp.inf, f32)
    by_sublane = lambda x: x.reshape(KEY_TILE // 8, 8, tq)

    def score_tile(q, i, chunk, slot, t, cmax):
        lo = chunk * key_chunk + t * KEY_TILE
        s = jnp.dot(k_ref[0, lo:lo + KEY_TILE, :], q, preferred_element_type=f32)
        s_ref[i, slot, t * KEY_TILE:(t + 1) * KEY_TILE, :] = s
        return jnp.maximum(cmax, jnp.max(by_sublane(s), axis=0))

    def q_tile(qi, cmax):
        qs = load_q(qi)
        qs_next = load_q(jnp.minimum(qi + 1, n_q - 1))
        cmax = list(cmax)
        m = [neg_inf((1, tq)) for _ in range(n_maps)]
        acc = [jnp.zeros((rows, tq), f32) for _ in range(n_maps)]
        pending = [None] * n_maps
        pv = [None] * n_maps

        def issue_pv(i):
            p, lo, alpha_c, last = pending[i]
            d = jnp.dot(vt_ref[0, 0, :, lo:lo + KEY_TILE], p, preferred_element_type=f32)
            pv[i] = d if pv[i] is None else pv[i] + d
            if last:
                acc[i] = alpha_c * acc[i] + pv[i]
                pv[i] = None
            pending[i] = None

        for c in range(n_chunks):
            m_new = [jnp.maximum(m[i], jnp.max(cmax[i], axis=0, keepdims=True)) for i in range(n_maps)]
            alpha = [jnp.exp2(m[i] - m_new[i]) for i in range(n_maps)]
            cmax_next = [neg_inf((8, tq)) for _ in range(n_maps)]
            for t in range(n_tiles):
                for i in range(n_maps):
                    if pv_first and pending[i] is not None:
                        issue_pv(i)
                    if c + 1 < n_chunks:
                        cmax_next[i] = score_tile(qs[i], i, c + 1, (c + 1) % 2, t, cmax_next[i])
                    else:
                        cmax_next[i] = score_tile(qs_next[i], i, 0, 0, t, cmax_next[i])
                    s = s_ref[i, c % 2, t * KEY_TILE:(t + 1) * KEY_TILE, :]
                    p = jnp.exp2(s - m_new[i]).astype(jnp.bfloat16)
                    if pending[i] is not None:
                        issue_pv(i)
                    pending[i] = (p, c * key_chunk + t * KEY_TILE, alpha[i], t == n_tiles - 1)
            m, cmax = m_new, cmax_next
        for i in range(n_maps):
            issue_pv(i)
        finish(qi, acc)
        return tuple(cmax)

    q0 = load_q(0)
    cmax0 = [neg_inf((8, tq)) for _ in range(n_maps)]
    for t in range(n_tiles):
        for i in range(n_maps):
            cmax0[i] = score_tile(q0[i], i, 0, 0, t, cmax0[i])
    lax.fori_loop(0, n_q, q_tile, tuple(cmax0))


def _mla_attn_kernel(k_ref, qt_ref, vt_ref, gt_ref, o_ref, s_ref, *, seq, n_q):
    f32 = jnp.float32
    tq = qt_ref.shape[-1]
    per_step = MLA_TILES_PER_STEP

    def load_q(qi):
        return [qt_ref[0, 0, per_step * qi + j] for j in range(per_step)]

    def finish(qi, accs):
        for j, acc in enumerate(accs):
            o = acc[:MLA_V] * (1.0 / acc[MLA_V:MLA_V + 1])
            tile = per_step * qi + j
            o_ref[0, 0, tile] = (o * gt_ref[0, 0, tile].astype(f32)).astype(o_ref.dtype)

    _pipelined_attention(k_ref, vt_ref, s_ref, load_q, finish, seq=seq, n_q=n_q // per_step, tq=tq,
                         n_maps=per_step, key_chunk=MLA_KEY_CHUNK, pv_first=True)


def _diff_attn_kernel(k_ref, qt_ref, vt_ref, gt_ref, lam_ref, gd_ref, o_ref, s_ref, *,
                      seq, n_q, lam_init):
    f32 = jnp.float32
    tq = qt_ref.shape[-1]
    lp = lam_ref[...].astype(f32)
    lam = (jnp.exp(jnp.sum(lp[0:1] * lp[1:2], axis=1, keepdims=True))
           - jnp.exp(jnp.sum(lp[2:3] * lp[3:4], axis=1, keepdims=True)) + lam_init)
    first = lax.broadcasted_iota(jnp.int32, (LANES, tq), 0) < DIFF_HEAD_DIM
    per_step = DIFF_TILES_PER_STEP

    def load_q(qi):
        maps = []
        for j in range(per_step):
            qt = qt_ref[0, 0, per_step * qi + j]
            zero = jnp.zeros_like(qt)
            maps += [jnp.where(first, qt, zero), jnp.where(first, zero, qt)]
        return maps

    def finish(qi, accs):
        dv = 2 * DIFF_HEAD_DIM
        for j in range(per_step):
            a1, a2 = accs[2 * j], accs[2 * j + 1]
            o = a1[:dv] * (1.0 / a1[dv:dv + 1]) - lam * (a2[:dv] * (1.0 / a2[dv:dv + 1]))
            o = o * lax.rsqrt(jnp.mean(o * o, axis=0, keepdims=True) + DIFF_RMS_EPS)
            o = o * gd_ref[...] * (1.0 - lam_init)
            tile = per_step * qi + j
            o_ref[0, 0, tile] = (o * gt_ref[0, 0, tile].astype(f32)).astype(o_ref.dtype)

    _pipelined_attention(k_ref, vt_ref, s_ref, load_q, finish, seq=seq, n_q=n_q // per_step, tq=tq,
                         n_maps=2 * per_step, key_chunk=DIFF_KEY_CHUNK, pv_first=False)


def _out_kernel(x_ref, za_ref, zb_ref, gm_ref, wa_ref, wb_ref, wo_ref, lg_ref, lb_ref, o_ref):
    bf16, f32 = jnp.bfloat16, jnp.float32
    tm = TOKEN_TILE

    def residual(j):
        rows = slice(j * tm, (j + 1) * tm)
        za = za_ref[0, :, j].astype(f32).reshape(MLA_WIDTH, tm).T.astype(bf16)
        zb = zb_ref[0, :, j].astype(f32).reshape(DIFF_WIDTH, tm).T.astype(bf16)
        ya = jnp.dot(za, wa_ref[...], preferred_element_type=f32)
        yb = jnp.dot(zb, wb_ref[...], preferred_element_type=f32)
        gm = gm_ref[0, rows, :].astype(f32)
        merged = gm[:, :D_MODEL] * ya + gm[:, D_MODEL:] * yb
        out = jnp.dot(merged.astype(bf16), wo_ref[...], preferred_element_type=f32)
        return DEEPNORM_ALPHA * x_ref[0, rows, :] + out

    def layer_norm(j, r):
        mu = jnp.mean(r, axis=-1, keepdims=True)
        d = r - mu
        var = jnp.mean(d * d, axis=-1, keepdims=True)
        o_ref[0, j * tm:(j + 1) * tm, :] = d * lax.rsqrt(var + LN_EPS) * lg_ref[...] + lb_ref[...]

    r_prev = residual(0)
    for j in range(1, OUT_STEP_TILES):
        r = residual(j)
        layer_norm(j - 1, r_prev)
        r_prev = r
    layer_norm(OUT_STEP_TILES - 1, r_prev)


def _split_cast_kernel(w_ref, o_ref):
    o_ref[0] = w_ref[0, :, _IN_REST:].astype(o_ref.dtype)


def _layer_spec(shape):
    return lambda l: pl.BlockSpec((None,) + tuple(shape), lambda *_: (l,) + (0,) * len(shape))


def _prepare_weights(p):
    bf16 = jnp.bfloat16
    w_in = p["w_in"]
    w_head = w_in[:, :, :_IN_REST].astype(bf16)
    kr_cols = jnp.pad(w_head[:, :, _IN_KR:], ((0, 0), (0, 0), (MLA_NOPE, LANES - MLA_NOPE - MLA_ROPE)))
    w_lat = jnp.concatenate([w_head[:, :, :_IN_KR], kr_cols], axis=2)
    rest_cols = w_in.shape[2] - _IN_REST
    w_rest = pl.pallas_call(
        _split_cast_kernel,
        grid=(DEPTH, D_MODEL // WEIGHT_ROWS),
        in_specs=[pl.BlockSpec((1, WEIGHT_ROWS, w_in.shape[2]), lambda l, r: (l, r, 0))],
        out_specs=pl.BlockSpec((1, WEIGHT_ROWS, rest_cols), lambda l, r: (l, r, 0)),
        out_shape=jax.ShapeDtypeStruct((DEPTH, D_MODEL, rest_cols), bf16),
        compiler_params=pltpu.CompilerParams(dimension_semantics=("arbitrary", "arbitrary"),
                                             vmem_limit_bytes=VMEM_LIMIT_BYTES),
        name="w_rest",
    )(w_in)
    wq = p["w_q_up"].reshape(DEPTH, MLA_Q_LORA, MLA_HEADS, MLA_NOPE + MLA_ROPE)
    wq = jnp.pad(wq, ((0, 0), (0, 0), (0, 0), (0, LANES - MLA_NOPE - MLA_ROPE)))
    wq = wq.reshape(DEPTH, MLA_Q_LORA, MLA_HEADS * LANES).astype(bf16)
    wkv = p["w_kv_up"].reshape(DEPTH, MLA_KV_LORA, MLA_HEADS, MLA_NOPE + MLA_V)
    wkn = jnp.pad(wkv[..., :MLA_NOPE], ((0, 0), (0, 0), (0, 0), (0, LANES - MLA_NOPE)))
    wkn = wkn.reshape(DEPTH, MLA_KV_LORA, MLA_HEADS * LANES).astype(bf16)
    wv = wkv[..., MLA_NOPE:].reshape(DEPTH, MLA_KV_LORA, MLA_WIDTH).astype(bf16)
    return dict(
        w_lat=w_lat, w_rest=w_rest, wq=wq, wkn=wkn, wv=wv,
        g_q=p["g_q"][:, None], g_kv=p["g_kv"][:, None], b_merge=p["b_merge"][:, None],
        diff_lambda=p["diff_lambda"], g_diff=p["g_diff"][:, :, None],
        wa=p["w_branch_a"].astype(bf16), wb=p["w_branch_b"].astype(bf16), wo=p["w_out"].astype(bf16),
        ln_gamma=p["ln_gamma"][:, None], ln_beta=p["ln_beta"][:, None])


def _layer(x, w, l, tab_a, tab_b):
    bf16, f32 = jnp.bfloat16, jnp.float32
    batch, seq, _ = x.shape
    tm = TOKEN_TILE
    n_q = seq // tm
    step = STEP_TILES * tm
    cparams = functools.partial(pltpu.CompilerParams, vmem_limit_bytes=VMEM_LIMIT_BYTES)

    va_rows = MLA_V + ONES_ROWS
    vb_rows = 2 * DIFF_HEAD_DIM + ONES_ROWS
    tok = lambda b, i: (b, i, 0)
    tile5 = lambda b, i: (b, 0, i, 0, 0)
    lane4 = lambda b, i: (b, 0, 0, i)
    tab_spec = pl.BlockSpec((3, step, LANES), lambda b, i: (0, i, 0))
    ka, qat, vat, gat, kb, qbt, vbt, gbt, gm = pl.pallas_call(
        _proj_kernel,
        grid=(batch, seq // step),
        in_specs=[
            pl.BlockSpec((1, step, D_MODEL), tok),
            _layer_spec(w["w_lat"].shape[1:])(l), _layer_spec(w["w_rest"].shape[1:])(l),
            _layer_spec(w["wq"].shape[1:])(l), _layer_spec(w["wkn"].shape[1:])(l),
            _layer_spec(w["wv"].shape[1:])(l),
            _layer_spec((1, MLA_Q_LORA))(l), _layer_spec((1, MLA_KV_LORA))(l),
            _layer_spec((1, 2 * D_MODEL))(l),
            tab_spec, tab_spec,
        ],
        out_specs=[
            pl.BlockSpec((1, step, MLA_HEADS * LANES), tok),
            pl.BlockSpec((1, MLA_HEADS, STEP_TILES, LANES, tm), tile5),
            pl.BlockSpec((1, MLA_HEADS, va_rows, step), lane4),
            pl.BlockSpec((1, MLA_HEADS, STEP_TILES, MLA_V, tm), tile5),
            pl.BlockSpec((1, step, DIFF_WIDTH), tok),
            pl.BlockSpec((1, DIFF_HEADS, STEP_TILES, LANES, tm), tile5),
            pl.BlockSpec((1, DIFF_HEADS, vb_rows, step), lane4),
            pl.BlockSpec((1, DIFF_HEADS, STEP_TILES, LANES, tm), tile5),
            pl.BlockSpec((1, step, 2 * D_MODEL), tok),
        ],
        out_shape=[
            jax.ShapeDtypeStruct((batch, seq, MLA_HEADS * LANES), bf16),
            jax.ShapeDtypeStruct((batch, MLA_HEADS, n_q, LANES, tm), bf16),
            jax.ShapeDtypeStruct((batch, MLA_HEADS, va_rows, seq), bf16),
            jax.ShapeDtypeStruct((batch, MLA_HEADS, n_q, MLA_V, tm), bf16),
            jax.ShapeDtypeStruct((batch, seq, DIFF_WIDTH), bf16),
            jax.ShapeDtypeStruct((batch, DIFF_HEADS, n_q, LANES, tm), bf16),
            jax.ShapeDtypeStruct((batch, DIFF_HEADS, vb_rows, seq), bf16),
            jax.ShapeDtypeStruct((batch, DIFF_HEADS, n_q, LANES, tm), bf16),
            jax.ShapeDtypeStruct((batch, seq, 2 * D_MODEL), bf16),
        ],
        compiler_params=cparams(dimension_semantics=("arbitrary", "arbitrary")),
        name="proj",
    )(x, w["w_lat"], w["w_rest"], w["wq"], w["wkn"], w["wv"], w["g_q"], w["g_kv"], w["b_merge"],
      tab_a, tab_b)

    head4 = lambda b, h: (b, h, 0, 0)
    head5 = lambda b, h: (b, h, 0, 0, 0)
    za = pl.pallas_call(
        functools.partial(_mla_attn_kernel, seq=seq, n_q=n_q),
        grid=(batch, MLA_HEADS),
        in_specs=[
            pl.BlockSpec((1, seq, LANES), lambda b, h: (b, 0, h)),
            pl.BlockSpec((1, 1, n_q, LANES, tm), head5),
            pl.BlockSpec((1, 1, va_rows, seq), head4),
            pl.BlockSpec((1, 1, n_q, MLA_V, tm), head5),
        ],
        out_specs=pl.BlockSpec((1, 1, n_q, MLA_V, tm), head5),
        out_shape=jax.ShapeDtypeStruct((batch, MLA_HEADS, n_q, MLA_V, tm), bf16),
        scratch_shapes=[pltpu.VMEM((MLA_TILES_PER_STEP, 2, MLA_KEY_CHUNK, tm), f32)],
        compiler_params=cparams(dimension_semantics=("arbitrary", "arbitrary")),
        name="mla_attn",
    )(ka, qat, vat, gat)

    lam_init = 0.8 - 0.6 * math.exp(-0.3 * l)
    zb = pl.pallas_call(
        functools.partial(_diff_attn_kernel, seq=seq, n_q=n_q, lam_init=lam_init),
        grid=(batch, DIFF_HEADS),
        in_specs=[
            pl.BlockSpec((1, seq, LANES), lambda b, h: (b, 0, h)),
            pl.BlockSpec((1, 1, n_q, LANES, tm), head5),
            pl.BlockSpec((1, 1, vb_rows, seq), head4),
            pl.BlockSpec((1, 1, n_q, LANES, tm), head5),
            _layer_spec((4, DIFF_HEAD_DIM))(l),
            _layer_spec((2 * DIFF_HEAD_DIM, 1))(l),
        ],
        out_specs=pl.BlockSpec((1, 1, n_q, LANES, tm), head5),
        out_shape=jax.ShapeDtypeStruct((batch, DIFF_HEADS, n_q, LANES, tm), bf16),
        scratch_shapes=[pltpu.VMEM((2 * DIFF_TILES_PER_STEP, 2, DIFF_KEY_CHUNK, tm), f32)],
        compiler_params=cparams(dimension_semantics=("arbitrary", "arbitrary")),
        name="diff_attn",
    )(kb, qbt, vbt, gbt, w["diff_lambda"], w["g_diff"])

    out_step = OUT_STEP_TILES * tm
    return pl.pallas_call(
        _out_kernel,
        grid=(batch, seq // out_step),
        in_specs=[
            pl.BlockSpec((1, out_step, D_MODEL), tok),
            pl.BlockSpec((1, MLA_HEADS, OUT_STEP_TILES, MLA_V, tm), tile5),
            pl.BlockSpec((1, DIFF_HEADS, OUT_STEP_TILES, LANES, tm), tile5),
            pl.BlockSpec((1, out_step, 2 * D_MODEL), tok),
            _layer_spec((MLA_WIDTH, D_MODEL))(l), _layer_spec((DIFF_WIDTH, D_MODEL))(l),
            _layer_spec((D_MODEL, D_MODEL))(l),
            _layer_spec((1, D_MODEL))(l), _layer_spec((1, D_MODEL))(l),
        ],
        out_specs=pl.BlockSpec((1, out_step, D_MODEL), tok),
        out_shape=jax.ShapeDtypeStruct((batch, seq, D_MODEL), f32),
        compiler_params=cparams(dimension_semantics=("arbitrary", "arbitrary")),
        name="out_proj",
    )(x, za, zb, gm, w["wa"], w["wb"], w["wo"], w["ln_gamma"], w["ln_beta"])


def kernel(x, w_in, g_q, w_q_up, g_kv, w_kv_up, diff_lambda, g_diff, w_branch_a, w_branch_b,
           b_merge, w_out, ln_gamma, ln_beta):
    weights = _prepare_weights(dict(
        w_in=w_in, g_q=g_q, w_q_up=w_q_up, g_kv=g_kv, w_kv_up=w_kv_up, diff_lambda=diff_lambda,
        g_diff=g_diff, w_branch_a=w_branch_a, w_branch_b=w_branch_b, b_merge=b_merge, w_out=w_out,
        ln_gamma=ln_gamma, ln_beta=ln_beta))
    seq = x.shape[1]
    tab_a = _rotary_tables(seq, MLA_ROPE, MLA_NOPE, LANES)
    tab_b = _rotary_tables(seq, DIFF_ROT, 0, DIFF_HEAD_DIM)
    for l in range(DEPTH):
        x = _layer(x, weights, l, tab_a, tab_b)
    return x
```

```python
import functools
import math

import jax
import jax.numpy as jnp
from jax import lax
from jax.experimental import pallas as pl
from jax.experimental.pallas import tpu as pltpu

D_MODEL = 1024
DEPTH = 2
MLA_HEADS = 8
MLA_Q_LORA = 256
MLA_KV_LORA = 128
MLA_NOPE = 64
MLA_ROPE = 32
MLA_V = 64
MLA_WIDTH = MLA_HEADS * MLA_V
DIFF_HEADS = 4
DIFF_HEAD_DIM = 64
DIFF_WIDTH = DIFF_HEADS * 2 * DIFF_HEAD_DIM
DIFF_ROT = DIFF_HEAD_DIM // 4
ROPE_THETA = 500000.0
DEEPNORM_ALPHA = (2 * DEPTH) ** 0.25
LN_EPS = 1e-5
RMS_EPS = 1e-6
DIFF_RMS_EPS = 1e-5
LOG2_E = math.log2(math.e)

LANES = 128
TOKEN_TILE = 256
STEP_TILES = 2
OUT_STEP_TILES = 4
WEIGHT_COLS = 512
KEY_TILE = 256
MLA_KEY_CHUNK = 256
DIFF_KEY_CHUNK = 512
ONES_ROWS = 16
MLA_TILES_PER_STEP = 4
DIFF_TILES_PER_STEP = 2
VMEM_LIMIT_BYTES = 56 * 1024 * 1024

_IN_KR = MLA_Q_LORA + MLA_KV_LORA
_IN_REST = _IN_KR + MLA_ROPE
_L_KV = MLA_Q_LORA
_L_KR = _L_KV + MLA_KV_LORA
_L_END = _L_KR + LANES
_R_GA = 0
_R_QD = _R_GA + MLA_WIDTH
_R_KD = _R_QD + DIFF_WIDTH
_R_VD = _R_KD + DIFF_WIDTH
_R_GB = _R_VD + DIFF_WIDTH
_R_GM = _R_GB + DIFF_WIDTH
_R_END = _R_GM + 2 * D_MODEL


def _rotary_tables(seq, rot_dim, first_lane, period):
    half = rot_dim // 2
    inv_freq = ROPE_THETA ** (-jnp.arange(half, dtype=jnp.float32) / half)
    ang = jnp.arange(seq, dtype=jnp.float32)[:, None] * inv_freq[None, :]
    cos, sin = jnp.cos(ang), jnp.sin(ang)
    lane = jnp.arange(LANES) % period - first_lane
    lo = (lane >= 0) & (lane < half)
    hi = (lane >= half) & (lane < rot_dim)
    idx = jnp.clip(jnp.where(hi, lane - half, lane), 0, half - 1)
    cos_l, sin_l = cos[:, idx], sin[:, idx]
    c = jnp.where((lo | hi)[None, :], cos_l, 1.0)
    s_lo = jnp.where(lo[None, :], -sin_l, 0.0)
    s_hi = jnp.where(hi[None, :], sin_l, 0.0)
    return jnp.stack([c, s_lo, s_hi]).astype(jnp.float32)


def _rotate(x, tab, half):
    up = pltpu.roll(x, LANES - half, 1)
    down = pltpu.roll(x, half, 1)
    return x * tab[0] + up * tab[1] + down * tab[2]


def _rms_norm(x, g, eps):
    return x * lax.rsqrt(jnp.mean(x * x, axis=-1, keepdims=True) + eps) * g


def _with_ones_rows(vt):
    heads, _, tokens = vt.shape
    return jnp.concatenate([vt, jnp.ones((heads, ONES_ROWS, tokens), vt.dtype)], axis=1)


def _proj_kernel(x_ref, wlat_ref, wrest_ref, wq_ref, wkn_ref, wv_ref, gq_ref, gkv_ref, bm_ref,
                 taba_ref, tabb_ref,
                 ka_ref, qat_ref, vat_ref, gat_ref, kb_ref, qbt_ref, vbt_ref, gbt_ref, gm_ref):
    bf16, f32 = jnp.bfloat16, jnp.float32
    tm = TOKEN_TILE
    mla_scale = LOG2_E / math.sqrt(MLA_NOPE + MLA_ROPE)
    diff_scale = LOG2_E / math.sqrt(DIFF_HEAD_DIM)

    for j in range(STEP_TILES):
        rows = slice(j * tm, (j + 1) * tm)
        xb = x_ref[0, rows, :].astype(bf16)
        tab_a = [taba_ref[i, rows, :] for i in range(3)]
        tab_b = [tabb_ref[i, rows, :] for i in range(3)]

        def proj(lo, hi):
            return jnp.dot(xb, wrest_ref[:, lo:hi], preferred_element_type=f32)

        lat = jnp.dot(xb, wlat_ref[...], preferred_element_type=f32)
        cq = _rms_norm(lat[:, :_L_KV], gq_ref[...], RMS_EPS).astype(bf16)
        ckv = _rms_norm(lat[:, _L_KV:_L_KR], gkv_ref[...], RMS_EPS).astype(bf16)
        k_rope = _rotate(lat[:, _L_KR:_L_END], tab_a, MLA_ROPE // 2)
        q_all = jnp.dot(cq, wq_ref[...], preferred_element_type=f32)
        k_all = jnp.dot(ckv, wkn_ref[...], preferred_element_type=f32)
        for h in range(MLA_HEADS):
            sl = slice(h * LANES, (h + 1) * LANES)
            q_h = _rotate(q_all[:, sl], tab_a, MLA_ROPE // 2) * mla_scale
            qat_ref[0, h, j] = q_h.T.astype(bf16)
            ka_ref[0, rows, sl] = (k_all[:, sl] + k_rope).astype(bf16)
        v = jnp.dot(ckv, wv_ref[...], preferred_element_type=f32)
        vat_ref[0, :, :, rows] = _with_ones_rows(v.T.reshape(MLA_HEADS, MLA_V, tm)).astype(bf16)
        ga = proj(_R_GA, _R_QD)
        gat_ref[0, :, j] = (ga * jax.nn.sigmoid(ga)).T.reshape(MLA_HEADS, MLA_V, tm).astype(bf16)

        qk_d = proj(_R_QD, _R_VD)
        for h in range(DIFF_HEADS):
            sl = slice(h * LANES, (h + 1) * LANES)
            q_h = _rotate(qk_d[:, sl], tab_b, DIFF_ROT // 2)
            qbt_ref[0, h, j] = (q_h * diff_scale).T.astype(bf16)
            k_h = _rotate(qk_d[:, DIFF_WIDTH + h * LANES:DIFF_WIDTH + (h + 1) * LANES], tab_b, DIFF_ROT // 2)
            kb_ref[0, rows, sl] = k_h.astype(bf16)
        vbt_ref[0, :, :, rows] = _with_ones_rows(proj(_R_VD, _R_GB).T.reshape(DIFF_HEADS, LANES, tm)).astype(bf16)
        gb = proj(_R_GB, _R_GM)
        gbt_ref[0, :, j] = (gb * jax.nn.sigmoid(gb)).T.reshape(DIFF_HEADS, LANES, tm).astype(bf16)

        gm_ref[0, rows, :] = jax.nn.sigmoid(proj(_R_GM, _R_END) + bm_ref[...]).astype(bf16)


def _pipelined_attention(k_ref, vt_ref, s_ref, load_q, finish, *, seq, n_q, tq, n_maps, key_chunk, pv_first):
    f32 = jnp.float32
    n_chunks = seq // key_chunk
    n_tiles = key_chunk // KEY_TILE
    assert n_chunks % 2 == 0
    rows = vt_ref.shape[2]
    neg_inf = lambda shape: jnp.full(shape, -jnp.inf, f32)
    by_sublane = lambda x: x.reshape(KEY_TILE // 8, 8, tq)

    def score_tile(q, i, chunk, slot, t, cmax):
        lo = chunk * key_chunk + t * KEY_TILE
        s = jnp.dot(k_ref[0, lo:lo + KEY_TILE, :], q, preferred_element_type=f32)
        s_ref[i, slot, t * KEY_TILE:(t + 1) * KEY_TILE, :] = s
        return jnp.maximum(cmax, jnp.max(by_sublane(s), axis=0))

    def q_tile(qi, cmax):
        qs = load_q(qi)
        qs_next = load_q(jnp.minimum(qi + 1, n_q - 1))
        cmax = list(cmax)
        m = [neg_inf((1, tq)) for _ in range(n_maps)]
        acc = [jnp.zeros((rows, tq), f32) for _ in range(n_maps)]
        pending = [None] * n_maps
        pv = [None] * n_maps

        def issue_pv(i):
            p, lo, alpha_c, last = pending[i]
            d = jnp.dot(vt_ref[0, 0, :, lo:lo + KEY_TILE], p, preferred_element_type=f32)
            pv[i] = d if pv[i] is None else pv[i] + d
            if last:
                acc[i] = alpha_c * acc[i] + pv[i]
                pv[i] = None
            pending[i] = None

        for c in range(n_chunks):
            m_new = [jnp.maximum(m[i], jnp.max(cmax[i], axis=0, keepdims=True)) for i in range(n_maps)]
            alpha = [jnp.exp2(m[i] - m_new[i]) for i in range(n_maps)]
            cmax_next = [neg_inf((8, tq)) for _ in range(n_maps)]
            for t in range(n_tiles):
                for i in range(n_maps):
                    if pv_first and pending[i] is not None:
                        issue_pv(i)
                    if c + 1 < n_chunks:
                        cmax_next[i] = score_tile(qs[i], i, c + 1, (c + 1) % 2, t, cmax_next[i])
                    else:
                        cmax_next[i] = score_tile(qs_next[i], i, 0, 0, t, cmax_next[i])
                    s = s_ref[i, c % 2, t * KEY_TILE:(t + 1) * KEY_TILE, :]
                    p = jnp.exp2(s - m_new[i]).astype(jnp.bfloat16)
                    if pending[i] is not None:
                        issue_pv(i)
                    pending[i] = (p, c * key_chunk + t * KEY_TILE, alpha[i], t == n_tiles - 1)
            m, cmax = m_new, cmax_next
        for i in range(n_maps):
            issue_pv(i)
        finish(qi, acc)
        return tuple(cmax)

    q0 = load_q(0)
    cmax0 = [neg_inf((8, tq)) for _ in range(n_maps)]
    for t in range(n_tiles):
        for i in range(n_maps):
            cmax0[i] = score_tile(q0[i], i, 0, 0, t, cmax0[i])
    lax.fori_loop(0, n_q, q_tile, tuple(cmax0))


def _mla_attn_kernel(k_ref, qt_ref, vt_ref, gt_ref, o_ref, s_ref, *, seq, n_q):
    f32 = jnp.float32
    tq = qt_ref.shape[-1]
    per_step = MLA_TILES_PER_STEP

    def load_q(qi):
        return [qt_ref[0, 0, per_step * qi + j] for j in range(per_step)]

    def finish(qi, accs):
        for j, acc in enumerate(accs):
            o = acc[:MLA_V] * (1.0 / acc[MLA_V:MLA_V + 1])
            tile = per_step * qi + j
            o_ref[0, 0, tile] = (o * gt_ref[0, 0, tile].astype(f32)).astype(o_ref.dtype)

    _pipelined_attention(k_ref, vt_ref, s_ref, load_q, finish, seq=seq, n_q=n_q // per_step, tq=tq,
                         n_maps=per_step, key_chunk=MLA_KEY_CHUNK, pv_first=True)


def _diff_attn_kernel(k_ref, qt_ref, vt_ref, gt_ref, lam_ref, gd_ref, o_ref, s_ref, *,
                      seq, n_q, lam_init):
    f32 = jnp.float32
    tq = qt_ref.shape[-1]
    lp = lam_ref[...].astype(f32)
    lam = (jnp.exp(jnp.sum(lp[0:1] * lp[1:2], axis=1, keepdims=True))
           - jnp.exp(jnp.sum(lp[2:3] * lp[3:4], axis=1, keepdims=True)) + lam_init)
    first = lax.broadcasted_iota(jnp.int32, (LANES, tq), 0) < DIFF_HEAD_DIM
    per_step = DIFF_TILES_PER_STEP

    def load_q(qi):
        maps = []
        for j in range(per_step):
            qt = qt_ref[0, 0, per_step * qi + j]
            zero = jnp.zeros_like(qt)
            maps += [jnp.where(first, qt, zero), jnp.where(first, zero, qt)]
        return maps

    def finish(qi, accs):
        dv = 2 * DIFF_HEAD_DIM
        for j in range(per_step):
            a1, a2 = accs[2 * j], accs[2 * j + 1]
            o = a1[:dv] * (1.0 / a1[dv:dv + 1]) - lam * (a2[:dv] * (1.0 / a2[dv:dv + 1]))
            o = o * lax.rsqrt(jnp.mean(o * o, axis=0, keepdims=True) + DIFF_RMS_EPS)
            o = o * gd_ref[...] * (1.0 - lam_init)
            tile = per_step * qi + j
            o_ref[0, 0, tile] = (o * gt_ref[0, 0, tile].astype(f32)).astype(o_ref.dtype)

    _pipelined_attention(k_ref, vt_ref, s_ref, load_q, finish, seq=seq, n_q=n_q // per_step, tq=tq,
                         n_maps=2 * per_step, key_chunk=DIFF_KEY_CHUNK, pv_first=False)


def _out_kernel(x_ref, za_ref, zb_ref, gm_ref, wa_ref, wb_ref, wo_ref, lg_ref, lb_ref, o_ref):
    bf16, f32 = jnp.bfloat16, jnp.float32
    tm = TOKEN_TILE

    def residual(j):
        rows = slice(j * tm, (j + 1) * tm)
        za = za_ref[0, :, j].astype(f32).reshape(MLA_WIDTH, tm).T.astype(bf16)
        zb = zb_ref[0, :, j].astype(f32).reshape(DIFF_WIDTH, tm).T.astype(bf16)
        ya = jnp.dot(za, wa_ref[...], preferred_element_type=f32)
        yb = jnp.dot(zb, wb_ref[...], preferred_element_type=f32)
        gm = gm_ref[0, rows, :].astype(f32)
        merged = gm[:, :D_MODEL] * ya + gm[:, D_MODEL:] * yb
        out = jnp.dot(merged.astype(bf16), wo_ref[...], preferred_element_type=f32)
        return DEEPNORM_ALPHA * x_ref[0, rows, :] + out

    def layer_norm(j, r):
        mu = jnp.mean(r, axis=-1, keepdims=True)
        d = r - mu
        var = jnp.mean(d * d, axis=-1, keepdims=True)
        o_ref[0, j * tm:(j + 1) * tm, :] = d * lax.rsqrt(var + LN_EPS) * lg_ref[...] + lb_ref[...]

    r_prev = residual(0)
    for j in range(1, OUT_STEP_TILES):
        r = residual(j)
        layer_norm(j - 1, r_prev)
        r_prev = r
    layer_norm(OUT_STEP_TILES - 1, r_prev)


def _transpose_cast_kernel(wt_ref, o_ref):
    o_ref[...] = wt_ref[0].T.astype(o_ref.dtype)


def _layer_spec(shape):
    return lambda l: pl.BlockSpec((None,) + tuple(shape), lambda *_: (l,) + (0,) * len(shape))


def _prepare_weights(p):
    bf16 = jnp.bfloat16
    w_in_t = jnp.swapaxes(p["w_in"], 1, 2)

    def columns(first, count, name):
        return pl.pallas_call(
            _transpose_cast_kernel,
            grid=(DEPTH, count // WEIGHT_COLS),
            in_specs=[pl.BlockSpec((pl.Element(1), pl.Element(WEIGHT_COLS), pl.Element(D_MODEL)),
                                   lambda l, j: (l, pl.multiple_of(first + j * WEIGHT_COLS, 8), 0))],
            out_specs=pl.BlockSpec((None, D_MODEL, WEIGHT_COLS), lambda l, j: (l, 0, j)),
            out_shape=jax.ShapeDtypeStruct((DEPTH, D_MODEL, count), bf16),
            compiler_params=pltpu.CompilerParams(dimension_semantics=("arbitrary", "arbitrary"),
                                                 vmem_limit_bytes=VMEM_LIMIT_BYTES),
            name=name,
        )(w_in_t)

    w_first = columns(0, WEIGHT_COLS, "w_first")
    kr_cols = jnp.pad(w_first[:, :, _IN_KR:_IN_REST], ((0, 0), (0, 0), (MLA_NOPE, LANES - MLA_NOPE - MLA_ROPE)))
    w_lat = jnp.concatenate([w_first[:, :, :_IN_KR], kr_cols], axis=2)
    w_rest = columns(_IN_REST, w_in_t.shape[1] - _IN_REST, "w_rest")
    wq = p["w_q_up"].reshape(DEPTH, MLA_Q_LORA, MLA_HEADS, MLA_NOPE + MLA_ROPE)
    wq = jnp.pad(wq, ((0, 0), (0, 0), (0, 0), (0, LANES - MLA_NOPE - MLA_ROPE)))
    wq = wq.reshape(DEPTH, MLA_Q_LORA, MLA_HEADS * LANES).astype(bf16)
    wkv = p["w_kv_up"].reshape(DEPTH, MLA_KV_LORA, MLA_HEADS, MLA_NOPE + MLA_V)
    wkn = jnp.pad(wkv[..., :MLA_NOPE], ((0, 0), (0, 0), (0, 0), (0, LANES - MLA_NOPE)))
    wkn = wkn.reshape(DEPTH, MLA_KV_LORA, MLA_HEADS * LANES).astype(bf16)
    wv = wkv[..., MLA_NOPE:].reshape(DEPTH, MLA_KV_LORA, MLA_WIDTH).astype(bf16)
    return dict(
        w_lat=w_lat, w_rest=w_rest, wq=wq, wkn=wkn, wv=wv,
        g_q=p["g_q"][:, None], g_kv=p["g_kv"][:, None], b_merge=p["b_merge"][:, None],
        diff_lambda=p["diff_lambda"], g_diff=p["g_diff"][:, :, None],
        wa=p["w_branch_a"].astype(bf16), wb=p["w_branch_b"].astype(bf16), wo=p["w_out"].astype(bf16),
        ln_gamma=p["ln_gamma"][:, None], ln_beta=p["ln_beta"][:, None])


def _layer(x, w, l, tab_a, tab_b):
    bf16, f32 = jnp.bfloat16, jnp.float32
    batch, seq, _ = x.shape
    tm = TOKEN_TILE
    n_q = seq // tm
    step = STEP_TILES * tm
    cparams = functools.partial(pltpu.CompilerParams, vmem_limit_bytes=VMEM_LIMIT_BYTES)

    va_rows = MLA_V + ONES_ROWS
    vb_rows = 2 * DIFF_HEAD_DIM + ONES_ROWS
    tok = lambda b, i: (b, i, 0)
    tile5 = lambda b, i: (b, 0, i, 0, 0)
    lane4 = lambda b, i: (b, 0, 0, i)
    tab_spec = pl.BlockSpec((3, step, LANES), lambda b, i: (0, i, 0))
    ka, qat, vat, gat, kb, qbt, vbt, gbt, gm = pl.pallas_call(
        _proj_kernel,
        grid=(batch, seq // step),
        in_specs=[
            pl.BlockSpec((1, step, D_MODEL), tok),
            _layer_spec(w["w_lat"].shape[1:])(l), _layer_spec(w["w_rest"].shape[1:])(l),
            _layer_spec(w["wq"].shape[1:])(l), _layer_spec(w["wkn"].shape[1:])(l),
            _layer_spec(w["wv"].shape[1:])(l),
            _layer_spec((1, MLA_Q_LORA))(l), _layer_spec((1, MLA_KV_LORA))(l),
            _layer_spec((1, 2 * D_MODEL))(l),
            tab_spec, tab_spec,
        ],
        out_specs=[
            pl.BlockSpec((1, step, MLA_HEADS * LANES), tok),
            pl.BlockSpec((1, MLA_HEADS, STEP_TILES, LANES, tm), tile5),
            pl.BlockSpec((1, MLA_HEADS, va_rows, step), lane4),
            pl.BlockSpec((1, MLA_HEADS, STEP_TILES, MLA_V, tm), tile5),
            pl.BlockSpec((1, step, DIFF_WIDTH), tok),
            pl.BlockSpec((1, DIFF_HEADS, STEP_TILES, LANES, tm), tile5),
            pl.BlockSpec((1, DIFF_HEADS, vb_rows, step), lane4),
            pl.BlockSpec((1, DIFF_HEADS, STEP_TILES, LANES, tm), tile5),
            pl.BlockSpec((1, step, 2 * D_MODEL), tok),
        ],
        out_shape=[
            jax.ShapeDtypeStruct((batch, seq, MLA_HEADS * LANES), bf16),
            jax.ShapeDtypeStruct((batch, MLA_HEADS, n_q, LANES, tm), bf16),
            jax.ShapeDtypeStruct((batch, MLA_HEADS, va_rows, seq), bf16),
            jax.ShapeDtypeStruct((batch, MLA_HEADS, n_q, MLA_V, tm), bf16),
            jax.ShapeDtypeStruct((batch, seq, DIFF_WIDTH), bf16),
            jax.ShapeDtypeStruct((batch, DIFF_HEADS, n_q, LANES, tm), bf16),
            jax.ShapeDtypeStruct((batch, DIFF_HEADS, vb_rows, seq), bf16),
            jax.ShapeDtypeStruct((batch, DIFF_HEADS, n_q, LANES, tm), bf16),
            jax.ShapeDtypeStruct((batch, seq, 2 * D_MODEL), bf16),
        ],
        compiler_params=cparams(dimension_semantics=("arbitrary", "arbitrary")),
        name="proj",
    )(x, w["w_lat"], w["w_rest"], w["wq"], w["wkn"], w["wv"], w["g_q"], w["g_kv"], w["b_merge"],
      tab_a, tab_b)

    head4 = lambda b, h: (b, h, 0, 0)
    head5 = lambda b, h: (b, h, 0, 0, 0)
    za = pl.pallas_call(
        functools.partial(_mla_attn_kernel, seq=seq, n_q=n_q),
        grid=(batch, MLA_HEADS),
        in_specs=[
            pl.BlockSpec((1, seq, LANES), lambda b, h: (b, 0, h)),
            pl.BlockSpec((1, 1, n_q, LANES, tm), head5),
            pl.BlockSpec((1, 1, va_rows, seq), head4),
            pl.BlockSpec((1, 1, n_q, MLA_V, tm), head5),
        ],
        out_specs=pl.BlockSpec((1, 1, n_q, MLA_V, tm), head5),
        out_shape=jax.ShapeDtypeStruct((batch, MLA_HEADS, n_q, MLA_V, tm), bf16),
        scratch_shapes=[pltpu.VMEM((MLA_TILES_PER_STEP, 2, MLA_KEY_CHUNK, tm), f32)],
        compiler_params=cparams(dimension_semantics=("arbitrary", "arbitrary")),
        name="mla_attn",
    )(ka, qat, vat, gat)

    lam_init = 0.8 - 0.6 * math.exp(-0.3 * l)
    zb = pl.pallas_call(
        functools.partial(_diff_attn_kernel, seq=seq, n_q=n_q, lam_init=lam_init),
        grid=(batch, DIFF_HEADS),
        in_specs=[
            pl.BlockSpec((1, seq, LANES), lambda b, h: (b, 0, h)),
            pl.BlockSpec((1, 1, n_q, LANES, tm), head5),
            pl.BlockSpec((1, 1, vb_rows, seq), head4),
            pl.BlockSpec((1, 1, n_q, LANES, tm), head5),
            _layer_spec((4, DIFF_HEAD_DIM))(l),
            _layer_spec((2 * DIFF_HEAD_DIM, 1))(l),
        ],
        out_specs=pl.BlockSpec((1, 1, n_q, LANES, tm), head5),
        out_shape=jax.ShapeDtypeStruct((batch, DIFF_HEADS, n_q, LANES, tm), bf16),
        scratch_shapes=[pltpu.VMEM((2 * DIFF_TILES_PER_STEP, 2, DIFF_KEY_CHUNK, tm), f32)],
        compiler_params=cparams(dimension_semantics=("arbitrary", "arbitrary")),
        name="diff_attn",
    )(kb, qbt, vbt, gbt, w["diff_lambda"], w["g_diff"])

    out_step = OUT_STEP_TILES * tm
    return pl.pallas_call(
        _out_kernel,
        grid=(batch, seq // out_step),
        in_specs=[
            pl.BlockSpec((1, out_step, D_MODEL), tok),
            pl.BlockSpec((1, MLA_HEADS, OUT_STEP_TILES, MLA_V, tm), tile5),
            pl.BlockSpec((1, DIFF_HEADS, OUT_STEP_TILES, LANES, tm), tile5),
            pl.BlockSpec((1, out_step, 2 * D_MODEL), tok),
            _layer_spec((MLA_WIDTH, D_MODEL))(l), _layer_spec((DIFF_WIDTH, D_MODEL))(l),
            _layer_spec((D_MODEL, D_MODEL))(l),
            _layer_spec((1, D_MODEL))(l), _layer_spec((1, D_MODEL))(l),
        ],
        out_specs=pl.BlockSpec((1, out_step, D_MODEL), tok),
        out_shape=jax.ShapeDtypeStruct((batch, seq, D_MODEL), f32),
        compiler_params=cparams(dimension_semantics=("arbitrary", "arbitrary")),
        name="out_proj",
    )(x, za, zb, gm, w["wa"], w["wb"], w["wo"], w["ln_gamma"], w["ln_beta"])


def kernel(x, w_in, g_q, w_q_up, g_kv, w_kv_up, diff_lambda, g_diff, w_branch_a, w_branch_b,
           b_merge, w_out, ln_gamma, ln_beta):
    weights = _prepare_weights(dict(
        w_in=w_in, g_q=g_q, w_q_up=w_q_up, g_kv=g_kv, w_kv_up=w_kv_up, diff_lambda=diff_lambda,
        g_diff=g_diff, w_branch_a=w_branch_a, w_branch_b=w_branch_b, b_merge=b_merge, w_out=w_out,
        ln_gamma=ln_gamma, ln_beta=ln_beta))
    seq = x.shape[1]
    tab_a = _rotary_tables(seq, MLA_ROPE, MLA_NOPE, LANES)
    tab_b = _rotary_tables(seq, DIFF_ROT, 0, DIFF_HEAD_DIM)
    for l in range(DEPTH):
        x = _layer(x, weights, l, tab_a, tab_b)
    return x
```

```python
import functools
import math

import jax
import jax.numpy as jnp
from jax import lax
from jax.experimental import pallas as pl
from jax.experimental.pallas import tpu as pltpu

D_MODEL = 1024
DEPTH = 2
MLA_HEADS = 8
MLA_Q_LORA = 256
MLA_KV_LORA = 128
MLA_NOPE = 64
MLA_ROPE = 32
MLA_V = 64
MLA_WIDTH = MLA_HEADS * MLA_V
DIFF_HEADS = 4
DIFF_HEAD_DIM = 64
DIFF_WIDTH = DIFF_HEADS * 2 * DIFF_HEAD_DIM
DIFF_ROT = DIFF_HEAD_DIM // 4
ROPE_THETA = 500000.0
DEEPNORM_ALPHA = (2 * DEPTH) ** 0.25
LN_EPS = 1e-5
RMS_EPS = 1e-6
DIFF_RMS_EPS = 1e-5
LOG2_E = math.log2(math.e)

LANES = 128
TOKEN_TILE = 256
STEP_TILES = 4
OUT_STEP_TILES = 4
WEIGHT_COLS = 512
KEY_TILE = 256
MLA_KEY_CHUNK = 256
DIFF_KEY_CHUNK = 512
ONES_ROWS = 16
MLA_TILES_PER_STEP = 4
DIFF_TILES_PER_STEP = 2
VMEM_LIMIT_BYTES = 56 * 1024 * 1024

_IN_KR = MLA_Q_LORA + MLA_KV_LORA
_IN_REST = _IN_KR + MLA_ROPE
_L_KV = MLA_Q_LORA
_L_KR = _L_KV + MLA_KV_LORA
_L_END = _L_KR + LANES
_R_GA = 0
_R_QD = _R_GA + MLA_WIDTH
_R_KD = _R_QD + DIFF_WIDTH
_R_VD = _R_KD + DIFF_WIDTH
_R_GB = _R_VD + DIFF_WIDTH
_R_END = _R_GB + DIFF_WIDTH
_IN_GM = _IN_REST + _R_END


def _rotary_tables(seq, rot_dim, first_lane, period):
    half = rot_dim // 2
    inv_freq = ROPE_THETA ** (-jnp.arange(half, dtype=jnp.float32) / half)
    ang = jnp.arange(seq, dtype=jnp.float32)[:, None] * inv_freq[None, :]
    cos, sin = jnp.cos(ang), jnp.sin(ang)
    lane = jnp.arange(LANES) % period - first_lane
    lo = (lane >= 0) & (lane < half)
    hi = (lane >= half) & (lane < rot_dim)
    idx = jnp.clip(jnp.where(hi, lane - half, lane), 0, half - 1)
    cos_l, sin_l = cos[:, idx], sin[:, idx]
    c = jnp.where((lo | hi)[None, :], cos_l, 1.0)
    s_lo = jnp.where(lo[None, :], -sin_l, 0.0)
    s_hi = jnp.where(hi[None, :], sin_l, 0.0)
    return jnp.stack([c, s_lo, s_hi]).astype(jnp.float32)


def _rotate(x, tab, half):
    up = pltpu.roll(x, LANES - half, 1)
    down = pltpu.roll(x, half, 1)
    return x * tab[0] + up * tab[1] + down * tab[2]


def _rms_norm(x, g, eps):
    return x * lax.rsqrt(jnp.mean(x * x, axis=-1, keepdims=True) + eps) * g


def _with_ones_rows(vt):
    heads, _, tokens = vt.shape
    return jnp.concatenate([vt, jnp.ones((heads, ONES_ROWS, tokens), vt.dtype)], axis=1)


def _proj_kernel(x_ref, wlat_ref, wrest_ref, wq_ref, wkn_ref, wv_ref, gq_ref, gkv_ref,
                 taba_ref, tabb_ref,
                 ka_ref, qat_ref, vat_ref, gat_ref, kb_ref, qbt_ref, vbt_ref, gbt_ref):
    bf16, f32 = jnp.bfloat16, jnp.float32
    tm = TOKEN_TILE
    mla_scale = LOG2_E / math.sqrt(MLA_NOPE + MLA_ROPE)
    diff_scale = LOG2_E / math.sqrt(DIFF_HEAD_DIM)

    for j in range(STEP_TILES):
        rows = slice(j * tm, (j + 1) * tm)
        xb = x_ref[0, rows, :].astype(bf16)
        tab_a = [taba_ref[i, rows, :] for i in range(3)]
        tab_b = [tabb_ref[i, rows, :] for i in range(3)]

        def proj(lo, hi):
            return jnp.dot(xb, wrest_ref[:, lo:hi], preferred_element_type=f32)

        lat = jnp.dot(xb, wlat_ref[...], preferred_element_type=f32)
        cq = _rms_norm(lat[:, :_L_KV], gq_ref[...], RMS_EPS).astype(bf16)
        ckv = _rms_norm(lat[:, _L_KV:_L_KR], gkv_ref[...], RMS_EPS).astype(bf16)
        k_rope = _rotate(lat[:, _L_KR:_L_END], tab_a, MLA_ROPE // 2)
        q_all = jnp.dot(cq, wq_ref[...], preferred_element_type=f32)
        k_all = jnp.dot(ckv, wkn_ref[...], preferred_element_type=f32)
        for h in range(MLA_HEADS):
            sl = slice(h * LANES, (h + 1) * LANES)
            q_h = _rotate(q_all[:, sl], tab_a, MLA_ROPE // 2) * mla_scale
            qat_ref[0, h, j] = q_h.astype(bf16).T
            ka_ref[0, rows, sl] = (k_all[:, sl] + k_rope).astype(bf16)
        v = jnp.dot(ckv, wv_ref[...], preferred_element_type=f32)
        vat_ref[0, :, :, rows] = _with_ones_rows(v.T.reshape(MLA_HEADS, MLA_V, tm)).astype(bf16)
        ga = proj(_R_GA, _R_QD)
        gat_ref[0, :, j] = (ga * jax.nn.sigmoid(ga)).astype(bf16).T.reshape(MLA_HEADS, MLA_V, tm)

        qk_d = proj(_R_QD, _R_VD)
        for h in range(DIFF_HEADS):
            sl = slice(h * LANES, (h + 1) * LANES)
            q_h = _rotate(qk_d[:, sl], tab_b, DIFF_ROT // 2)
            qbt_ref[0, h, j] = (q_h * diff_scale).astype(bf16).T
            k_h = _rotate(qk_d[:, DIFF_WIDTH + h * LANES:DIFF_WIDTH + (h + 1) * LANES], tab_b, DIFF_ROT // 2)
            kb_ref[0, rows, sl] = k_h.astype(bf16)
        vbt_ref[0, :, :, rows] = _with_ones_rows(proj(_R_VD, _R_GB).T.reshape(DIFF_HEADS, LANES, tm)).astype(bf16)
        gb = proj(_R_GB, _R_END)
        gbt_ref[0, :, j] = (gb * jax.nn.sigmoid(gb)).astype(bf16).T.reshape(DIFF_HEADS, LANES, tm)


def _pipelined_attention(k_ref, vt_ref, s_ref, load_q, finish, *, seq, n_q, tq, n_maps, key_chunk, pv_first):
    f32 = jnp.float32
    n_chunks = seq // key_chunk
    n_tiles = key_chunk // KEY_TILE
    assert n_chunks % 2 == 0
    rows = vt_ref.shape[2]
    neg_inf = lambda shape: jnp.full(shape, -jnp.inf, f32)
    by_sublane = lambda x: x.reshape(KEY_TILE // 8, 8, tq)

    def score_tile(q, i, chunk, slot, t, cmax):
        lo = chunk * key_chunk + t * KEY_TILE
        s = jnp.dot(k_ref[0, lo:lo + KEY_TILE, :], q, preferred_element_type=f32)
        s_ref[i, slot, t * KEY_TILE:(t + 1) * KEY_TILE, :] = s
        return jnp.maximum(cmax, jnp.max(by_sublane(s), axis=0))

    def q_tile(qi, cmax):
        qs = load_q(qi)
        qs_next = load_q(jnp.minimum(qi + 1, n_q - 1))
        cmax = list(cmax)
        m = [neg_inf((1, tq)) for _ in range(n_maps)]
        acc = [jnp.zeros((rows, tq), f32) for _ in range(n_maps)]
        pending = [None] * n_maps
        pv = [None] * n_maps

        def issue_pv(i):
            p, lo, alpha_c, last = pending[i]
            d = jnp.dot(vt_ref[0, 0, :, lo:lo + KEY_TILE], p, preferred_element_type=f32)
            pv[i] = d if pv[i] is None else pv[i] + d
            if last:
                acc[i] = alpha_c * acc[i] + pv[i]
                pv[i] = None
            pending[i] = None

        for c in range(n_chunks):
            m_new = [jnp.maximum(m[i], jnp.max(cmax[i], axis=0, keepdims=True)) for i in range(n_maps)]
            alpha = [jnp.exp2(m[i] - m_new[i]) for i in range(n_maps)]
            cmax_next = [neg_inf((8, tq)) for _ in range(n_maps)]
            for t in range(n_tiles):
                for i in range(n_maps):
                    if pv_first and pending[i] is not None:
                        issue_pv(i)
                    if c + 1 < n_chunks:
                        cmax_next[i] = score_tile(qs[i], i, c + 1, (c + 1) % 2, t, cmax_next[i])
                    else:
                        cmax_next[i] = score_tile(qs_next[i], i, 0, 0, t, cmax_next[i])
                    s = s_ref[i, c % 2, t * KEY_TILE:(t + 1) * KEY_TILE, :]
                    p = jnp.exp2(s - m_new[i]).astype(jnp.bfloat16)
                    if pending[i] is not None:
                        issue_pv(i)
                    pending[i] = (p, c * key_chunk + t * KEY_TILE, alpha[i], t == n_tiles - 1)
            m, cmax = m_new, cmax_next
        for i in range(n_maps):
            issue_pv(i)
        finish(qi, acc)
        return tuple(cmax)

    q0 = load_q(0)
    cmax0 = [neg_inf((8, tq)) for _ in range(n_maps)]
    for t in range(n_tiles):
        for i in range(n_maps):
            cmax0[i] = score_tile(q0[i], i, 0, 0, t, cmax0[i])
    lax.fori_loop(0, n_q, q_tile, tuple(cmax0))


def _mla_attn_kernel(k_ref, qt_ref, vt_ref, gt_ref, o_ref, s_ref, *, seq, n_q):
    f32 = jnp.float32
    tq = qt_ref.shape[-1]
    per_step = MLA_TILES_PER_STEP

    def load_q(qi):
        return [qt_ref[0, 0, per_step * qi + j] for j in range(per_step)]

    def finish(qi, accs):
        for j, acc in enumerate(accs):
            o = acc[:MLA_V] * (1.0 / acc[MLA_V:MLA_V + 1])
            tile = per_step * qi + j
            o_ref[0, 0, tile] = (o * gt_ref[0, 0, tile].astype(f32)).astype(o_ref.dtype)

    _pipelined_attention(k_ref, vt_ref, s_ref, load_q, finish, seq=seq, n_q=n_q // per_step, tq=tq,
                         n_maps=per_step, key_chunk=MLA_KEY_CHUNK, pv_first=True)


def _diff_attn_kernel(k_ref, qt_ref, vt_ref, gt_ref, lam_ref, gd_ref, o_ref, s_ref, *,
                      seq, n_q, lam_init):
    f32 = jnp.float32
    tq = qt_ref.shape[-1]
    lp = lam_ref[...].astype(f32)
    lam = (jnp.exp(jnp.sum(lp[0:1] * lp[1:2], axis=1, keepdims=True))
           - jnp.exp(jnp.sum(lp[2:3] * lp[3:4], axis=1, keepdims=True)) + lam_init)
    first = lax.broadcasted_iota(jnp.int32, (LANES, tq), 0) < DIFF_HEAD_DIM
    per_step = DIFF_TILES_PER_STEP

    def load_q(qi):
        maps = []
        for j in range(per_step):
            qt = qt_ref[0, 0, per_step * qi + j]
            zero = jnp.zeros_like(qt)
            maps += [jnp.where(first, qt, zero), jnp.where(first, zero, qt)]
        return maps

    def finish(qi, accs):
        dv = 2 * DIFF_HEAD_DIM
        for j in range(per_step):
            a1, a2 = accs[2 * j], accs[2 * j + 1]
            o = a1[:dv] * (1.0 / a1[dv:dv + 1]) - lam * (a2[:dv] * (1.0 / a2[dv:dv + 1]))
            o = o * lax.rsqrt(jnp.mean(o * o, axis=0, keepdims=True) + DIFF_RMS_EPS)
            o = o * gd_ref[...] * (1.0 - lam_init)
            tile = per_step * qi + j
            o_ref[0, 0, tile] = (o * gt_ref[0, 0, tile].astype(f32)).astype(o_ref.dtype)

    _pipelined_attention(k_ref, vt_ref, s_ref, load_q, finish, seq=seq, n_q=n_q // per_step, tq=tq,
                         n_maps=2 * per_step, key_chunk=DIFF_KEY_CHUNK, pv_first=False)


def _out_kernel(x_ref, za_ref, zb_ref, wgm_ref, bm_ref, wa_ref, wb_ref, wo_ref, lg_ref, lb_ref, o_ref):
    bf16, f32 = jnp.bfloat16, jnp.float32
    tm = TOKEN_TILE

    def residual(j):
        rows = slice(j * tm, (j + 1) * tm)
        za = za_ref[0, :, j].astype(f32).reshape(MLA_WIDTH, tm).T.astype(bf16)
        zb = zb_ref[0, :, j].astype(f32).reshape(DIFF_WIDTH, tm).T.astype(bf16)
        ya = jnp.dot(za, wa_ref[...], preferred_element_type=f32)
        yb = jnp.dot(zb, wb_ref[...], preferred_element_type=f32)
        x = x_ref[0, rows, :]
        gm = jax.nn.sigmoid(jnp.dot(x.astype(bf16), wgm_ref[...], preferred_element_type=f32) + bm_ref[...])
        merged = gm[:, :D_MODEL] * ya + gm[:, D_MODEL:] * yb
        out = jnp.dot(merged.astype(bf16), wo_ref[...], preferred_element_type=f32)
        return DEEPNORM_ALPHA * x + out

    def layer_norm(j, r):
        mu = jnp.mean(r, axis=-1, keepdims=True)
        d = r - mu
        var = jnp.mean(d * d, axis=-1, keepdims=True)
        o_ref[0, j * tm:(j + 1) * tm, :] = d * lax.rsqrt(var + LN_EPS) * lg_ref[...] + lb_ref[...]

    r_prev = residual(0)
    for j in range(1, OUT_STEP_TILES):
        r = residual(j)
        layer_norm(j - 1, r_prev)
        r_prev = r
    layer_norm(OUT_STEP_TILES - 1, r_prev)


def _transpose_cast_kernel(wt_ref, o_ref):
    o_ref[...] = wt_ref[0].T.astype(o_ref.dtype)


def _layer_spec(shape):
    return lambda l: pl.BlockSpec((None,) + tuple(shape), lambda *_: (l,) + (0,) * len(shape))


def _prepare_weights(p):
    bf16 = jnp.bfloat16
    w_in_t = jnp.swapaxes(p["w_in"], 1, 2)

    def columns(first, count, name):
        return pl.pallas_call(
            _transpose_cast_kernel,
            grid=(DEPTH, count // WEIGHT_COLS),
            in_specs=[pl.BlockSpec((pl.Element(1), pl.Element(WEIGHT_COLS), pl.Element(D_MODEL)),
                                   lambda l, j: (l, pl.multiple_of(first + j * WEIGHT_COLS, 8), 0))],
            out_specs=pl.BlockSpec((None, D_MODEL, WEIGHT_COLS), lambda l, j: (l, 0, j)),
            out_shape=jax.ShapeDtypeStruct((DEPTH, D_MODEL, count), bf16),
            compiler_params=pltpu.CompilerParams(dimension_semantics=("arbitrary", "arbitrary"),
                                                 vmem_limit_bytes=VMEM_LIMIT_BYTES),
            name=name,
        )(w_in_t)

    w_first = columns(0, WEIGHT_COLS, "w_first")
    kr_cols = jnp.pad(w_first[:, :, _IN_KR:_IN_REST], ((0, 0), (0, 0), (MLA_NOPE, LANES - MLA_NOPE - MLA_ROPE)))
    w_lat = jnp.concatenate([w_first[:, :, :_IN_KR], kr_cols], axis=2)
    w_rest = columns(_IN_REST, _R_END, "w_rest")
    w_gm = columns(_IN_GM, 2 * D_MODEL, "w_gm")
    wq = p["w_q_up"].reshape(DEPTH, MLA_Q_LORA, MLA_HEADS, MLA_NOPE + MLA_ROPE)
    wq = jnp.pad(wq, ((0, 0), (0, 0), (0, 0), (0, LANES - MLA_NOPE - MLA_ROPE)))
    wq = wq.reshape(DEPTH, MLA_Q_LORA, MLA_HEADS * LANES).astype(bf16)
    wkv = p["w_kv_up"].reshape(DEPTH, MLA_KV_LORA, MLA_HEADS, MLA_NOPE + MLA_V)
    wkn = jnp.pad(wkv[..., :MLA_NOPE], ((0, 0), (0, 0), (0, 0), (0, LANES - MLA_NOPE)))
    wkn = wkn.reshape(DEPTH, MLA_KV_LORA, MLA_HEADS * LANES).astype(bf16)
    wv = wkv[..., MLA_NOPE:].reshape(DEPTH, MLA_KV_LORA, MLA_WIDTH).astype(bf16)
    return dict(
        w_lat=w_lat, w_rest=w_rest, w_gm=w_gm, wq=wq, wkn=wkn, wv=wv,
        g_q=p["g_q"][:, None], g_kv=p["g_kv"][:, None], b_merge=p["b_merge"][:, None],
        diff_lambda=p["diff_lambda"], g_diff=p["g_diff"][:, :, None],
        wa=p["w_branch_a"].astype(bf16), wb=p["w_branch_b"].astype(bf16), wo=p["w_out"].astype(bf16),
        ln_gamma=p["ln_gamma"][:, None], ln_beta=p["ln_beta"][:, None])


def _layer(x, w, l, tab_a, tab_b):
    bf16, f32 = jnp.bfloat16, jnp.float32
    batch, seq, _ = x.shape
    tm = TOKEN_TILE
    n_q = seq // tm
    step = STEP_TILES * tm
    cparams = functools.partial(pltpu.CompilerParams, vmem_limit_bytes=VMEM_LIMIT_BYTES)

    va_rows = MLA_V + ONES_ROWS
    vb_rows = 2 * DIFF_HEAD_DIM + ONES_ROWS
    tok = lambda b, i: (b, i, 0)
    tile5 = lambda b, i: (b, 0, i, 0, 0)
    lane4 = lambda b, i: (b, 0, 0, i)
    tab_spec = pl.BlockSpec((3, step, LANES), lambda b, i: (0, i, 0))
    ka, qat, vat, gat, kb, qbt, vbt, gbt = pl.pallas_call(
        _proj_kernel,
        grid=(batch, seq // step),
        in_specs=[
            pl.BlockSpec((1, step, D_MODEL), tok),
            _layer_spec(w["w_lat"].shape[1:])(l), _layer_spec(w["w_rest"].shape[1:])(l),
            _layer_spec(w["wq"].shape[1:])(l), _layer_spec(w["wkn"].shape[1:])(l),
            _layer_spec(w["wv"].shape[1:])(l),
            _layer_spec((1, MLA_Q_LORA))(l), _layer_spec((1, MLA_KV_LORA))(l),
            tab_spec, tab_spec,
        ],
        out_specs=[
            pl.BlockSpec((1, step, MLA_HEADS * LANES), tok),
            pl.BlockSpec((1, MLA_HEADS, STEP_TILES, LANES, tm), tile5),
            pl.BlockSpec((1, MLA_HEADS, va_rows, step), lane4),
            pl.BlockSpec((1, MLA_HEADS, STEP_TILES, MLA_V, tm), tile5),
            pl.BlockSpec((1, step, DIFF_WIDTH), tok),
            pl.BlockSpec((1, DIFF_HEADS, STEP_TILES, LANES, tm), tile5),
            pl.BlockSpec((1, DIFF_HEADS, vb_rows, step), lane4),
            pl.BlockSpec((1, DIFF_HEADS, STEP_TILES, LANES, tm), tile5),
        ],
        out_shape=[
            jax.ShapeDtypeStruct((batch, seq, MLA_HEADS * LANES), bf16),
            jax.ShapeDtypeStruct((batch, MLA_HEADS, n_q, LANES, tm), bf16),
            jax.ShapeDtypeStruct((batch, MLA_HEADS, va_rows, seq), bf16),
            jax.ShapeDtypeStruct((batch, MLA_HEADS, n_q, MLA_V, tm), bf16),
            jax.ShapeDtypeStruct((batch, seq, DIFF_WIDTH), bf16),
            jax.ShapeDtypeStruct((batch, DIFF_HEADS, n_q, LANES, tm), bf16),
            jax.ShapeDtypeStruct((batch, DIFF_HEADS, vb_rows, seq), bf16),
            jax.ShapeDtypeStruct((batch, DIFF_HEADS, n_q, LANES, tm), bf16),
        ],
        compiler_params=cparams(dimension_semantics=("arbitrary", "arbitrary")),
        name="proj",
    )(x, w["w_lat"], w["w_rest"], w["wq"], w["wkn"], w["wv"], w["g_q"], w["g_kv"], tab_a, tab_b)

    head4 = lambda b, h: (b, h, 0, 0)
    head5 = lambda b, h: (b, h, 0, 0, 0)
    za = pl.pallas_call(
        functools.partial(_mla_attn_kernel, seq=seq, n_q=n_q),
        grid=(batch, MLA_HEADS),
        in_specs=[
            pl.BlockSpec((1, seq, LANES), lambda b, h: (b, 0, h)),
            pl.BlockSpec((1, 1, n_q, LANES, tm), head5),
            pl.BlockSpec((1, 1, va_rows, seq), head4),
            pl.BlockSpec((1, 1, n_q, MLA_V, tm), head5),
        ],
        out_specs=pl.BlockSpec((1, 1, n_q, MLA_V, tm), head5),
        out_shape=jax.ShapeDtypeStruct((batch, MLA_HEADS, n_q, MLA_V, tm), bf16),
        scratch_shapes=[pltpu.VMEM((MLA_TILES_PER_STEP, 2, MLA_KEY_CHUNK, tm), f32)],
        compiler_params=cparams(dimension_semantics=("arbitrary", "arbitrary")),
        name="mla_attn",
    )(ka, qat, vat, gat)

    lam_init = 0.8 - 0.6 * math.exp(-0.3 * l)
    zb = pl.pallas_call(
        functools.partial(_diff_attn_kernel, seq=seq, n_q=n_q, lam_init=lam_init),
        grid=(batch, DIFF_HEADS),
        in_specs=[
            pl.BlockSpec((1, seq, LANES), lambda b, h: (b, 0, h)),
            pl.BlockSpec((1, 1, n_q, LANES, tm), head5),
            pl.BlockSpec((1, 1, vb_rows, seq), head4),
            pl.BlockSpec((1, 1, n_q, LANES, tm), head5),
            _layer_spec((4, DIFF_HEAD_DIM))(l),
            _layer_spec((2 * DIFF_HEAD_DIM, 1))(l),
        ],
        out_specs=pl.BlockSpec((1, 1, n_q, LANES, tm), head5),
        out_shape=jax.ShapeDtypeStruct((batch, DIFF_HEADS, n_q, LANES, tm), bf16),
        scratch_shapes=[pltpu.VMEM((2 * DIFF_TILES_PER_STEP, 2, DIFF_KEY_CHUNK, tm), f32)],
        compiler_params=cparams(dimension_semantics=("arbitrary", "arbitrary")),
        name="diff_attn",
    )(kb, qbt, vbt, gbt, w["diff_lambda"], w["g_diff"])

    out_step = OUT_STEP_TILES * tm
    return pl.pallas_call(
        _out_kernel,
        grid=(batch, seq // out_step),
        in_specs=[
            pl.BlockSpec((1, out_step, D_MODEL), tok),
            pl.BlockSpec((1, MLA_HEADS, OUT_STEP_TILES, MLA_V, tm), tile5),
            pl.BlockSpec((1, DIFF_HEADS, OUT_STEP_TILES, LANES, tm), tile5),
            _layer_spec((D_MODEL, 2 * D_MODEL))(l), _layer_spec((1, 2 * D_MODEL))(l),
            _layer_spec((MLA_WIDTH, D_MODEL))(l), _layer_spec((DIFF_WIDTH, D_MODEL))(l),
            _layer_spec((D_MODEL, D_MODEL))(l),
            _layer_spec((1, D_MODEL))(l), _layer_spec((1, D_MODEL))(l),
        ],
        out_specs=pl.BlockSpec((1, out_step, D_MODEL), tok),
        out_shape=jax.ShapeDtypeStruct((batch, seq, D_MODEL), f32),
        compiler_params=cparams(dimension_semantics=("arbitrary", "arbitrary")),
        name="out_proj",
    )(x, za, zb, w["w_gm"], w["b_merge"], w["wa"], w["wb"], w["wo"], w["ln_gamma"], w["ln_beta"])


def kernel(x, w_in, g_q, w_q_up, g_kv, w_kv_up, diff_lambda, g_diff, w_branch_a, w_branch_b,
           b_merge, w_out, ln_gamma, ln_beta):
    weights = _prepare_weights(dict(
        w_in=w_in, g_q=g_q, w_q_up=w_q_up, g_kv=g_kv, w_kv_up=w_kv_up, diff_lambda=diff_lambda,
        g_diff=g_diff, w_branch_a=w_branch_a, w_branch_b=w_branch_b, b_merge=b_merge, w_out=w_out,
        ln_gamma=ln_gamma, ln_beta=ln_beta))
    seq = x.shape[1]
    tab_a = _rotary_tables(seq, MLA_ROPE, MLA_NOPE, LANES)
    tab_b = _rotary_tables(seq, DIFF_ROT, 0, DIFF_HEAD_DIM)
    for l in range(DEPTH):
        x = _layer(x, weights, l, tab_a, tab_b)
    return x
```

```python
import functools
import math

import jax
import jax.numpy as jnp
from jax import lax
from jax.experimental import pallas as pl
from jax.experimental.pallas import tpu as pltpu

D_MODEL = 1024
DEPTH = 2
MLA_HEADS = 8
MLA_Q_LORA = 256
MLA_KV_LORA = 128
MLA_NOPE = 64
MLA_ROPE = 32
MLA_V = 64
MLA_WIDTH = MLA_HEADS * MLA_V
DIFF_HEADS = 4
DIFF_HEAD_DIM = 64
DIFF_WIDTH = DIFF_HEADS * 2 * DIFF_HEAD_DIM
DIFF_ROT = DIFF_HEAD_DIM // 4
ROPE_THETA = 500000.0
DEEPNORM_ALPHA = (2 * DEPTH) ** 0.25
LN_EPS = 1e-5
RMS_EPS = 1e-6
DIFF_RMS_EPS = 1e-5
LOG2_E = math.log2(math.e)

LANES = 128
TOKEN_TILE = 256
STEP_TILES = 4
OUT_STEP_TILES = 4
WEIGHT_COLS = 512
KEY_TILE = 256
MLA_KEY_CHUNK = 256
DIFF_KEY_CHUNK = 512
ONES_ROWS = 16
MLA_HEADS_PER_STEP = 4
DIFF_HEADS_PER_STEP = 2
MLA_TILES_PER_STEP = 4
DIFF_TILES_PER_STEP = 2
VMEM_LIMIT_BYTES = 56 * 1024 * 1024

_IN_KR = MLA_Q_LORA + MLA_KV_LORA
_IN_REST = _IN_KR + MLA_ROPE
_L_KV = MLA_Q_LORA
_L_KR = _L_KV + MLA_KV_LORA
_L_END = _L_KR + LANES
_R_GA = 0
_R_QD = _R_GA + MLA_WIDTH
_R_KD = _R_QD + DIFF_WIDTH
_R_VD = _R_KD + DIFF_WIDTH
_R_GB = _R_VD + DIFF_WIDTH
_R_END = _R_GB + DIFF_WIDTH
_IN_GM = _IN_REST + _R_END


def _rotary_tables(seq, rot_dim, first_lane, period):
    half = rot_dim // 2
    inv_freq = ROPE_THETA ** (-jnp.arange(half, dtype=jnp.float32) / half)
    ang = jnp.arange(seq, dtype=jnp.float32)[:, None] * inv_freq[None, :]
    cos, sin = jnp.cos(ang), jnp.sin(ang)
    lane = jnp.arange(LANES) % period - first_lane
    lo = (lane >= 0) & (lane < half)
    hi = (lane >= half) & (lane < rot_dim)
    idx = jnp.clip(jnp.where(hi, lane - half, lane), 0, half - 1)
    cos_l, sin_l = cos[:, idx], sin[:, idx]
    c = jnp.where((lo | hi)[None, :], cos_l, 1.0)
    s_lo = jnp.where(lo[None, :], -sin_l, 0.0)
    s_hi = jnp.where(hi[None, :], sin_l, 0.0)
    return jnp.stack([c, s_lo, s_hi]).astype(jnp.float32)


def _rotate(x, tab, half):
    up = pltpu.roll(x, LANES - half, 1)
    down = pltpu.roll(x, half, 1)
    return x * tab[0] + up * tab[1] + down * tab[2]


def _rms_norm(x, g, eps):
    return x * lax.rsqrt(jnp.mean(x * x, axis=-1, keepdims=True) + eps) * g


def _with_ones_rows(vt):
    heads, _, tokens = vt.shape
    return jnp.concatenate([vt, jnp.ones((heads, ONES_ROWS, tokens), vt.dtype)], axis=1)


def _proj_kernel(x_ref, wlat_ref, wrest_ref, wq_ref, wkn_ref, wv_ref, gq_ref, gkv_ref,
                 taba_ref, tabb_ref,
                 ka_ref, qat_ref, vat_ref, gat_ref, kb_ref, qbt_ref, vbt_ref, gbt_ref):
    bf16, f32 = jnp.bfloat16, jnp.float32
    tm = TOKEN_TILE
    mla_scale = LOG2_E / math.sqrt(MLA_NOPE + MLA_ROPE)
    diff_scale = LOG2_E / math.sqrt(DIFF_HEAD_DIM)

    for j in range(STEP_TILES):
        rows = slice(j * tm, (j + 1) * tm)
        xb = x_ref[0, rows, :].astype(bf16)
        tab_a = [taba_ref[i, rows, :] for i in range(3)]
        tab_b = [tabb_ref[i, rows, :] for i in range(3)]

        def proj(lo, hi):
            return jnp.dot(xb, wrest_ref[:, lo:hi], preferred_element_type=f32)

        lat = jnp.dot(xb, wlat_ref[...], preferred_element_type=f32)
        cq = _rms_norm(lat[:, :_L_KV], gq_ref[...], RMS_EPS).astype(bf16)
        ckv = _rms_norm(lat[:, _L_KV:_L_KR], gkv_ref[...], RMS_EPS).astype(bf16)
        k_rope = _rotate(lat[:, _L_KR:_L_END], tab_a, MLA_ROPE // 2)
        q_all = jnp.dot(cq, wq_ref[...], preferred_element_type=f32)
        k_all = jnp.dot(ckv, wkn_ref[...], preferred_element_type=f32)
        for h in range(MLA_HEADS):
            sl = slice(h * LANES, (h + 1) * LANES)
            q_h = _rotate(q_all[:, sl], tab_a, MLA_ROPE // 2) * mla_scale
            qat_ref[0, h, j] = q_h.astype(bf16).T
            ka_ref[0, h, rows, :] = (k_all[:, sl] + k_rope).astype(bf16)
        v = jnp.dot(ckv, wv_ref[...], preferred_element_type=f32)
        vat_ref[0, :, :, rows] = _with_ones_rows(v.T.reshape(MLA_HEADS, MLA_V, tm)).astype(bf16)
        ga = proj(_R_GA, _R_QD)
        gat_ref[0, :, j] = (ga * jax.nn.sigmoid(ga)).astype(bf16).T.reshape(MLA_HEADS, MLA_V, tm)

        qk_d = proj(_R_QD, _R_VD)
        for h in range(DIFF_HEADS):
            sl = slice(h * LANES, (h + 1) * LANES)
            q_h = _rotate(qk_d[:, sl], tab_b, DIFF_ROT // 2)
            qbt_ref[0, h, j] = (q_h * diff_scale).astype(bf16).T
            k_h = _rotate(qk_d[:, DIFF_WIDTH + h * LANES:DIFF_WIDTH + (h + 1) * LANES], tab_b, DIFF_ROT // 2)
            kb_ref[0, h, rows, :] = k_h.astype(bf16)
        vbt_ref[0, :, :, rows] = _with_ones_rows(proj(_R_VD, _R_GB).T.reshape(DIFF_HEADS, LANES, tm)).astype(bf16)
        gb = proj(_R_GB, _R_END)
        gbt_ref[0, :, j] = (gb * jax.nn.sigmoid(gb)).astype(bf16).T.reshape(DIFF_HEADS, LANES, tm)


def _pipelined_attention(k_ref, vt_ref, s_ref, load_q, finish, *, seq, n_q, tq, n_maps, key_chunk, pv_first):
    f32 = jnp.float32
    n_chunks = seq // key_chunk
    n_tiles = key_chunk // KEY_TILE
    assert n_chunks % 2 == 0
    n_steps = k_ref.shape[1] * n_q
    rows = vt_ref.shape[2]
    neg_inf = lambda shape: jnp.full(shape, -jnp.inf, f32)
    by_sublane = lambda x: x.reshape(KEY_TILE // 8, 8, tq)

    def score_tile(q, i, head, chunk, slot, t, cmax):
        lo = chunk * key_chunk + t * KEY_TILE
        s = jnp.dot(k_ref[0, head, lo:lo + KEY_TILE, :], q, preferred_element_type=f32)
        s_ref[i, slot, t * KEY_TILE:(t + 1) * KEY_TILE, :] = s
        return jnp.maximum(cmax, jnp.max(by_sublane(s), axis=0))

    def q_tile(step, cmax):
        head, qi = lax.div(step, n_q), lax.rem(step, n_q)
        step_next = jnp.minimum(step + 1, n_steps - 1)
        head_next = lax.div(step_next, n_q)
        qs = load_q(head, qi)
        qs_next = load_q(head_next, lax.rem(step_next, n_q))
        cmax = list(cmax)
        m = [neg_inf((1, tq)) for _ in range(n_maps)]
        acc = [jnp.zeros((rows, tq), f32) for _ in range(n_maps)]
        pending = [None] * n_maps
        pv = [None] * n_maps

        def issue_pv(i):
            p, lo, alpha_c, last = pending[i]
            d = jnp.dot(vt_ref[0, head, :, lo:lo + KEY_TILE], p, preferred_element_type=f32)
            pv[i] = d if pv[i] is None else pv[i] + d
            if last:
                acc[i] = alpha_c * acc[i] + pv[i]
                pv[i] = None
            pending[i] = None

        for c in range(n_chunks):
            m_new = [jnp.maximum(m[i], jnp.max(cmax[i], axis=0, keepdims=True)) for i in range(n_maps)]
            alpha = [jnp.exp2(m[i] - m_new[i]) for i in range(n_maps)]
            cmax_next = [neg_inf((8, tq)) for _ in range(n_maps)]
            for t in range(n_tiles):
                for i in range(n_maps):
                    if pv_first and pending[i] is not None:
                        issue_pv(i)
                    if c + 1 < n_chunks:
                        cmax_next[i] = score_tile(qs[i], i, head, c + 1, (c + 1) % 2, t, cmax_next[i])
                    else:
                        cmax_next[i] = score_tile(qs_next[i], i, head_next, 0, 0, t, cmax_next[i])
                    s = s_ref[i, c % 2, t * KEY_TILE:(t + 1) * KEY_TILE, :]
                    p = jnp.exp2(s - m_new[i]).astype(jnp.bfloat16)
                    if pending[i] is not None:
                        issue_pv(i)
                    pending[i] = (p, c * key_chunk + t * KEY_TILE, alpha[i], t == n_tiles - 1)
            m, cmax = m_new, cmax_next
        for i in range(n_maps):
            issue_pv(i)
        finish(head, qi, acc)
        return tuple(cmax)

    q0 = load_q(0, 0)
    cmax0 = [neg_inf((8, tq)) for _ in range(n_maps)]
    for t in range(n_tiles):
        for i in range(n_maps):
            cmax0[i] = score_tile(q0[i], i, 0, 0, 0, t, cmax0[i])
    lax.fori_loop(0, n_steps, q_tile, tuple(cmax0))


def _mla_attn_kernel(k_ref, qt_ref, vt_ref, gt_ref, o_ref, s_ref, *, seq, n_q):
    f32 = jnp.float32
    tq = qt_ref.shape[-1]
    per_step = MLA_TILES_PER_STEP

    def load_q(head, qi):
        return [qt_ref[0, head, per_step * qi + j] for j in range(per_step)]

    def finish(head, qi, accs):
        for j, acc in enumerate(accs):
            o = acc[:MLA_V] * (1.0 / acc[MLA_V:MLA_V + 1])
            tile = per_step * qi + j
            o_ref[0, head, tile] = (o * gt_ref[0, head, tile].astype(f32)).astype(o_ref.dtype)

    _pipelined_attention(k_ref, vt_ref, s_ref, load_q, finish, seq=seq, n_q=n_q // per_step, tq=tq,
                         n_maps=per_step, key_chunk=MLA_KEY_CHUNK, pv_first=True)


def _diff_attn_kernel(k_ref, qt_ref, vt_ref, gt_ref, lam_ref, gd_ref, o_ref, s_ref, *,
                      seq, n_q, lam_init):
    f32 = jnp.float32
    tq = qt_ref.shape[-1]
    lp = lam_ref[...].astype(f32)
    lam = (jnp.exp(jnp.sum(lp[0:1] * lp[1:2], axis=1, keepdims=True))
           - jnp.exp(jnp.sum(lp[2:3] * lp[3:4], axis=1, keepdims=True)) + lam_init)
    first = lax.broadcasted_iota(jnp.int32, (LANES, tq), 0) < DIFF_HEAD_DIM
    per_step = DIFF_TILES_PER_STEP

    def load_q(head, qi):
        maps = []
        for j in range(per_step):
            qt = qt_ref[0, head, per_step * qi + j]
            zero = jnp.zeros_like(qt)
            maps += [jnp.where(first, qt, zero), jnp.where(first, zero, qt)]
        return maps

    def finish(head, qi, accs):
        dv = 2 * DIFF_HEAD_DIM
        for j in range(per_step):
            a1, a2 = accs[2 * j], accs[2 * j + 1]
            o = a1[:dv] * (1.0 / a1[dv:dv + 1]) - lam * (a2[:dv] * (1.0 / a2[dv:dv + 1]))
            o = o * lax.rsqrt(jnp.mean(o * o, axis=0, keepdims=True) + DIFF_RMS_EPS)
            o = o * gd_ref[...] * (1.0 - lam_init)
            tile = per_step * qi + j
            o_ref[0, head, tile] = (o * gt_ref[0, head, tile].astype(f32)).astype(o_ref.dtype)

    _pipelined_attention(k_ref, vt_ref, s_ref, load_q, finish, seq=seq, n_q=n_q // per_step, tq=tq,
                         n_maps=2 * per_step, key_chunk=DIFF_KEY_CHUNK, pv_first=False)


def _out_kernel(x_ref, za_ref, zb_ref, wgm_ref, bm_ref, wa_ref, wb_ref, wo_ref, lg_ref, lb_ref, o_ref):
    bf16, f32 = jnp.bfloat16, jnp.float32
    tm = TOKEN_TILE

    def residual(j):
        rows = slice(j * tm, (j + 1) * tm)
        za = za_ref[0, :, j].astype(f32).reshape(MLA_WIDTH, tm).T.astype(bf16)
        zb = zb_ref[0, :, j].astype(f32).reshape(DIFF_WIDTH, tm).T.astype(bf16)
        ya = jnp.dot(za, wa_ref[...], preferred_element_type=f32)
        yb = jnp.dot(zb, wb_ref[...], preferred_element_type=f32)
        x = x_ref[0, rows, :]
        gm = jax.nn.sigmoid(jnp.dot(x.astype(bf16), wgm_ref[...], preferred_element_type=f32) + bm_ref[...])
        merged = gm[:, :D_MODEL] * ya + gm[:, D_MODEL:] * yb
        out = jnp.dot(merged.astype(bf16), wo_ref[...], preferred_element_type=f32)
        return DEEPNORM_ALPHA * x + out

    def layer_norm(j, r):
        mu = jnp.mean(r, axis=-1, keepdims=True)
        d = r - mu
        var = jnp.mean(d * d, axis=-1, keepdims=True)
        o_ref[0, j * tm:(j + 1) * tm, :] = d * lax.rsqrt(var + LN_EPS) * lg_ref[...] + lb_ref[...]

    r_prev = residual(0)
    for j in range(1, OUT_STEP_TILES):
        r = residual(j)
        layer_norm(j - 1, r_prev)
        r_prev = r
    layer_norm(OUT_STEP_TILES - 1, r_prev)


def _transpose_cast_kernel(wt_ref, o_ref):
    o_ref[...] = wt_ref[0].T.astype(o_ref.dtype)


def _layer_spec(shape):
    return lambda l: pl.BlockSpec((None,) + tuple(shape), lambda *_: (l,) + (0,) * len(shape))


def _prepare_weights(p):
    bf16 = jnp.bfloat16
    w_in_t = jnp.swapaxes(p["w_in"], 1, 2)

    def columns(first, count, name):
        return pl.pallas_call(
            _transpose_cast_kernel,
            grid=(DEPTH, count // WEIGHT_COLS),
            in_specs=[pl.BlockSpec((pl.Element(1), pl.Element(WEIGHT_COLS), pl.Element(D_MODEL)),
                                   lambda l, j: (l, pl.multiple_of(first + j * WEIGHT_COLS, 8), 0))],
            out_specs=pl.BlockSpec((None, D_MODEL, WEIGHT_COLS), lambda l, j: (l, 0, j)),
            out_shape=jax.ShapeDtypeStruct((DEPTH, D_MODEL, count), bf16),
            compiler_params=pltpu.CompilerParams(dimension_semantics=("arbitrary", "arbitrary"),
                                                 vmem_limit_bytes=VMEM_LIMIT_BYTES),
            name=name,
        )(w_in_t)

    w_first = columns(0, WEIGHT_COLS, "w_first")
    kr_cols = jnp.pad(w_first[:, :, _IN_KR:_IN_REST], ((0, 0), (0, 0), (MLA_NOPE, LANES - MLA_NOPE - MLA_ROPE)))
    w_lat = jnp.concatenate([w_first[:, :, :_IN_KR], kr_cols], axis=2)
    w_rest = columns(_IN_REST, _R_END, "w_rest")
    w_gm = columns(_IN_GM, 2 * D_MODEL, "w_gm")
    wq = p["w_q_up"].reshape(DEPTH, MLA_Q_LORA, MLA_HEADS, MLA_NOPE + MLA_ROPE)
    wq = jnp.pad(wq, ((0, 0), (0, 0), (0, 0), (0, LANES - MLA_NOPE - MLA_ROPE)))
    wq = wq.reshape(DEPTH, MLA_Q_LORA, MLA_HEADS * LANES).astype(bf16)
    wkv = p["w_kv_up"].reshape(DEPTH, MLA_KV_LORA, MLA_HEADS, MLA_NOPE + MLA_V)
    wkn = jnp.pad(wkv[..., :MLA_NOPE], ((0, 0), (0, 0), (0, 0), (0, LANES - MLA_NOPE)))
    wkn = wkn.reshape(DEPTH, MLA_KV_LORA, MLA_HEADS * LANES).astype(bf16)
    wv = wkv[..., MLA_NOPE:].reshape(DEPTH, MLA_KV_LORA, MLA_WIDTH).astype(bf16)
    return dict(
        w_lat=w_lat, w_rest=w_rest, w_gm=w_gm, wq=wq, wkn=wkn, wv=wv,
        g_q=p["g_q"][:, None], g_kv=p["g_kv"][:, None], b_merge=p["b_merge"][:, None],
        diff_lambda=p["diff_lambda"], g_diff=p["g_diff"][:, :, None],
        wa=p["w_branch_a"].astype(bf16), wb=p["w_branch_b"].astype(bf16), wo=p["w_out"].astype(bf16),
        ln_gamma=p["ln_gamma"][:, None], ln_beta=p["ln_beta"][:, None])


def _layer(x, w, l, tab_a, tab_b):
    bf16, f32 = jnp.bfloat16, jnp.float32
    batch, seq, _ = x.shape
    tm = TOKEN_TILE
    n_q = seq // tm
    step = STEP_TILES * tm
    cparams = functools.partial(pltpu.CompilerParams, vmem_limit_bytes=VMEM_LIMIT_BYTES)

    va_rows = MLA_V + ONES_ROWS
    vb_rows = 2 * DIFF_HEAD_DIM + ONES_ROWS
    tok = lambda b, i: (b, i, 0)
    tile5 = lambda b, i: (b, 0, i, 0, 0)
    lane4 = lambda b, i: (b, 0, 0, i)
    tab_spec = pl.BlockSpec((3, step, LANES), lambda b, i: (0, i, 0))
    ka, qat, vat, gat, kb, qbt, vbt, gbt = pl.pallas_call(
        _proj_kernel,
        grid=(batch, seq // step),
        in_specs=[
            pl.BlockSpec((1, step, D_MODEL), tok),
            _layer_spec(w["w_lat"].shape[1:])(l), _layer_spec(w["w_rest"].shape[1:])(l),
            _layer_spec(w["wq"].shape[1:])(l), _layer_spec(w["wkn"].shape[1:])(l),
            _layer_spec(w["wv"].shape[1:])(l),
            _layer_spec((1, MLA_Q_LORA))(l), _layer_spec((1, MLA_KV_LORA))(l),
            tab_spec, tab_spec,
        ],
        out_specs=[
            pl.BlockSpec((1, MLA_HEADS, step, LANES), lambda b, i: (b, 0, i, 0)),
            pl.BlockSpec((1, MLA_HEADS, STEP_TILES, LANES, tm), tile5),
            pl.BlockSpec((1, MLA_HEADS, va_rows, step), lane4),
            pl.BlockSpec((1, MLA_HEADS, STEP_TILES, MLA_V, tm), tile5),
            pl.BlockSpec((1, DIFF_HEADS, step, LANES), lambda b, i: (b, 0, i, 0)),
            pl.BlockSpec((1, DIFF_HEADS, STEP_TILES, LANES, tm), tile5),
            pl.BlockSpec((1, DIFF_HEADS, vb_rows, step), lane4),
            pl.BlockSpec((1, DIFF_HEADS, STEP_TILES, LANES, tm), tile5),
        ],
        out_shape=[
            jax.ShapeDtypeStruct((batch, MLA_HEADS, seq, LANES), bf16),
            jax.ShapeDtypeStruct((batch, MLA_HEADS, n_q, LANES, tm), bf16),
            jax.ShapeDtypeStruct((batch, MLA_HEADS, va_rows, seq), bf16),
            jax.ShapeDtypeStruct((batch, MLA_HEADS, n_q, MLA_V, tm), bf16),
            jax.ShapeDtypeStruct((batch, DIFF_HEADS, seq, LANES), bf16),
            jax.ShapeDtypeStruct((batch, DIFF_HEADS, n_q, LANES, tm), bf16),
            jax.ShapeDtypeStruct((batch, DIFF_HEADS, vb_rows, seq), bf16),
            jax.ShapeDtypeStruct((batch, DIFF_HEADS, n_q, LANES, tm), bf16),
        ],
        compiler_params=cparams(dimension_semantics=("arbitrary", "arbitrary")),
        name="proj",
    )(x, w["w_lat"], w["w_rest"], w["wq"], w["wkn"], w["wv"], w["g_q"], w["g_kv"], tab_a, tab_b)

    head4 = lambda b, h: (b, h, 0, 0)
    head5 = lambda b, h: (b, h, 0, 0, 0)
    za = pl.pallas_call(
        functools.partial(_mla_attn_kernel, seq=seq, n_q=n_q),
        grid=(batch, MLA_HEADS // MLA_HEADS_PER_STEP),
        in_specs=[
            pl.BlockSpec((1, MLA_HEADS_PER_STEP, seq, LANES), head4),
            pl.BlockSpec((1, MLA_HEADS_PER_STEP, n_q, LANES, tm), head5),
            pl.BlockSpec((1, MLA_HEADS_PER_STEP, va_rows, seq), head4),
            pl.BlockSpec((1, MLA_HEADS_PER_STEP, n_q, MLA_V, tm), head5),
        ],
        out_specs=pl.BlockSpec((1, MLA_HEADS_PER_STEP, n_q, MLA_V, tm), head5),
        out_shape=jax.ShapeDtypeStruct((batch, MLA_HEADS, n_q, MLA_V, tm), bf16),
        scratch_shapes=[pltpu.VMEM((MLA_TILES_PER_STEP, 2, MLA_KEY_CHUNK, tm), f32)],
        compiler_params=cparams(dimension_semantics=("arbitrary", "arbitrary")),
        name="mla_attn",
    )(ka, qat, vat, gat)

    lam_init = 0.8 - 0.6 * math.exp(-0.3 * l)
    zb = pl.pallas_call(
        functools.partial(_diff_attn_kernel, seq=seq, n_q=n_q, lam_init=lam_init),
        grid=(batch, DIFF_HEADS // DIFF_HEADS_PER_STEP),
        in_specs=[
            pl.BlockSpec((1, DIFF_HEADS_PER_STEP, seq, LANES), head4),
            pl.BlockSpec((1, DIFF_HEADS_PER_STEP, n_q, LANES, tm), head5),
            pl.BlockSpec((1, DIFF_HEADS_PER_STEP, vb_rows, seq), head4),
            pl.BlockSpec((1, DIFF_HEADS_PER_STEP, n_q, LANES, tm), head5),
            _layer_spec((4, DIFF_HEAD_DIM))(l),
            _layer_spec((2 * DIFF_HEAD_DIM, 1))(l),
        ],
        out_specs=pl.BlockSpec((1, DIFF_HEADS_PER_STEP, n_q, LANES, tm), head5),
        out_shape=jax.ShapeDtypeStruct((batch, DIFF_HEADS, n_q, LANES, tm), bf16),
        scratch_shapes=[pltpu.VMEM((2 * DIFF_TILES_PER_STEP, 2, DIFF_KEY_CHUNK, tm), f32)],
        compiler_params=cparams(dimension_semantics=("arbitrary", "arbitrary")),
        name="diff_attn",
    )(kb, qbt, vbt, gbt, w["diff_lambda"], w["g_diff"])

    out_step = OUT_STEP_TILES * tm
    return pl.pallas_call(
        _out_kernel,
        grid=(batch, seq // out_step),
        in_specs=[
            pl.BlockSpec((1, out_step, D_MODEL), tok),
            pl.BlockSpec((1, MLA_HEADS, OUT_STEP_TILES, MLA_V, tm), tile5),
            pl.BlockSpec((1, DIFF_HEADS, OUT_STEP_TILES, LANES, tm), tile5),
            _layer_spec((D_MODEL, 2 * D_MODEL))(l), _layer_spec((1, 2 * D_MODEL))(l),
            _layer_spec((MLA_WIDTH, D_MODEL))(l), _layer_spec((DIFF_WIDTH, D_MODEL))(l),
            _layer_spec((D_MODEL, D_MODEL))(l),
            _layer_spec((1, D_MODEL))(l), _layer_spec((1, D_MODEL))(l),
        ],
        out_specs=pl.BlockSpec((1, out_step, D_MODEL), tok),
        out_shape=jax.ShapeDtypeStruct((batch, seq, D_MODEL), f32),
        compiler_params=cparams(dimension_semantics=("arbitrary", "arbitrary")),
        name="out_proj",
    )(x, za, zb, w["w_gm"], w["b_merge"], w["wa"], w["wb"], w["wo"], w["ln_gamma"], w["ln_beta"])


def kernel(x, w_in, g_q, w_q_up, g_kv, w_kv_up, diff_lambda, g_diff, w_branch_a, w_branch_b,
           b_merge, w_out, ln_gamma, ln_beta):
    weights = _prepare_weights(dict(
        w_in=w_in, g_q=g_q, w_q_up=w_q_up, g_kv=g_kv, w_kv_up=w_kv_up, diff_lambda=diff_lambda,
        g_diff=g_diff, w_branch_a=w_branch_a, w_branch_b=w_branch_b, b_merge=b_merge, w_out=w_out,
        ln_gamma=ln_gamma, ln_beta=ln_beta))
    seq = x.shape[1]
    tab_a = _rotary_tables(seq, MLA_ROPE, MLA_NOPE, LANES)
    tab_b = _rotary_tables(seq, DIFF_ROT, 0, DIFF_HEAD_DIM)
    for l in range(DEPTH):
        x = _layer(x, weights, l, tab_a, tab_b)
    return x
```

```python
import functools
import math

import jax
import jax.numpy as jnp
from jax import lax
from jax.experimental import pallas as pl
from jax.experimental.pallas import tpu as pltpu

D_MODEL = 1024
DEPTH = 2
MLA_HEADS = 8
MLA_Q_LORA = 256
MLA_KV_LORA = 128
MLA_NOPE = 64
MLA_ROPE = 32
MLA_V = 64
MLA_WIDTH = MLA_HEADS * MLA_V
DIFF_HEADS = 4
DIFF_HEAD_DIM = 64
DIFF_WIDTH = DIFF_HEADS * 2 * DIFF_HEAD_DIM
DIFF_ROT = DIFF_HEAD_DIM // 4
ROPE_THETA = 500000.0
DEEPNORM_ALPHA = (2 * DEPTH) ** 0.25
LN_EPS = 1e-5
RMS_EPS = 1e-6
DIFF_RMS_EPS = 1e-5
LOG2_E = math.log2(math.e)

LANES = 128
TOKEN_TILE = 256
STEP_TILES = 4
OUT_STEP_TILES = 4
WEIGHT_COLS = 512
KEY_TILE = 256
MLA_KEY_CHUNK = 256
DIFF_KEY_CHUNK = 512
ONES_ROWS = 16
MLA_HEADS_PER_STEP = 4
DIFF_HEADS_PER_STEP = 2
MLA_TILES_PER_STEP = 4
DIFF_TILES_PER_STEP = 2
VMEM_LIMIT_BYTES = 56 * 1024 * 1024

_IN_KR = MLA_Q_LORA + MLA_KV_LORA
_IN_REST = _IN_KR + MLA_ROPE
_L_KV = MLA_Q_LORA
_L_KR = _L_KV + MLA_KV_LORA
_L_END = _L_KR + LANES
_R_GA = 0
_R_QD = _R_GA + MLA_WIDTH
_R_KD = _R_QD + DIFF_WIDTH
_R_VD = _R_KD + DIFF_WIDTH
_R_GB = _R_VD + DIFF_WIDTH
_R_END = _R_GB + DIFF_WIDTH
_IN_GM = _IN_REST + _R_END


def _rotary_tables(seq, rot_dim, first_lane, period):
    half = rot_dim // 2
    lane = jnp.arange(LANES) % period - first_lane
    lo = (lane >= 0) & (lane < half)
    hi = (lane >= half) & (lane < rot_dim)
    idx = jnp.clip(jnp.where(hi, lane - half, lane), 0, half - 1)
    inv_freq = ROPE_THETA ** (-idx.astype(jnp.float32) / half)
    ang = jnp.arange(seq, dtype=jnp.float32)[:, None] * inv_freq[None, :]
    cos, sin = jnp.cos(ang), jnp.sin(ang)
    c = jnp.where((lo | hi)[None, :], cos, 1.0)
    s_lo = jnp.where(lo[None, :], -sin, 0.0)
    s_hi = jnp.where(hi[None, :], sin, 0.0)
    return jnp.stack([c, s_lo, s_hi]).astype(jnp.float32)


def _rotate(x, tab, half):
    up = pltpu.roll(x, LANES - half, 1)
    down = pltpu.roll(x, half, 1)
    return x * tab[0] + up * tab[1] + down * tab[2]


def _rms_norm(x, g, eps):
    return x * lax.rsqrt(jnp.mean(x * x, axis=-1, keepdims=True) + eps) * g


def _with_ones_rows(vt):
    heads, _, tokens = vt.shape
    return jnp.concatenate([vt, jnp.ones((heads, ONES_ROWS, tokens), vt.dtype)], axis=1)


def _proj_kernel(x_ref, wlat_ref, wrest_ref, wq_ref, wkn_ref, wv_ref, gq_ref, gkv_ref,
                 taba_ref, tabb_ref,
                 ka_ref, qat_ref, vat_ref, gat_ref, kb_ref, qbt_ref, vbt_ref, gbt_ref):
    bf16, f32 = jnp.bfloat16, jnp.float32
    tm = TOKEN_TILE
    mla_scale = LOG2_E / math.sqrt(MLA_NOPE + MLA_ROPE)
    diff_scale = LOG2_E / math.sqrt(DIFF_HEAD_DIM)

    for j in range(STEP_TILES):
        rows = slice(j * tm, (j + 1) * tm)
        xb = x_ref[0, rows, :].astype(bf16)
        tab_a = [taba_ref[i, rows, :] for i in range(3)]
        tab_b = [tabb_ref[i, rows, :] for i in range(3)]

        def proj(lo, hi):
            return jnp.dot(xb, wrest_ref[:, lo:hi], preferred_element_type=f32)

        lat = jnp.dot(xb, wlat_ref[...], preferred_element_type=f32)
        cq = _rms_norm(lat[:, :_L_KV], gq_ref[...], RMS_EPS).astype(bf16)
        ckv = _rms_norm(lat[:, _L_KV:_L_KR], gkv_ref[...], RMS_EPS).astype(bf16)
        k_rope = _rotate(lat[:, _L_KR:_L_END], tab_a, MLA_ROPE // 2)
        q_all = jnp.dot(cq, wq_ref[...], preferred_element_type=f32)
        k_all = jnp.dot(ckv, wkn_ref[...], preferred_element_type=f32)
        for h in range(MLA_HEADS):
            sl = slice(h * LANES, (h + 1) * LANES)
            q_h = _rotate(q_all[:, sl], tab_a, MLA_ROPE // 2) * mla_scale
            qat_ref[0, h, j] = q_h.astype(bf16).T
            ka_ref[0, h, rows, :] = (k_all[:, sl] + k_rope).astype(bf16)
        v = jnp.dot(ckv, wv_ref[...], preferred_element_type=f32)
        vat_ref[0, :, :, rows] = _with_ones_rows(v.T.reshape(MLA_HEADS, MLA_V, tm)).astype(bf16)
        ga = proj(_R_GA, _R_QD)
        gat_ref[0, :, j] = (ga * jax.nn.sigmoid(ga)).astype(bf16).T.reshape(MLA_HEADS, MLA_V, tm)

        qk_d = proj(_R_QD, _R_VD)
        for h in range(DIFF_HEADS):
            sl = slice(h * LANES, (h + 1) * LANES)
            q_h = _rotate(qk_d[:, sl], tab_b, DIFF_ROT // 2)
            qbt_ref[0, h, j] = (q_h * diff_scale).astype(bf16).T
            k_h = _rotate(qk_d[:, DIFF_WIDTH + h * LANES:DIFF_WIDTH + (h + 1) * LANES], tab_b, DIFF_ROT // 2)
            kb_ref[0, h, rows, :] = k_h.astype(bf16)
        vbt_ref[0, :, :, rows] = _with_ones_rows(proj(_R_VD, _R_GB).T.reshape(DIFF_HEADS, LANES, tm)).astype(bf16)
        gb = proj(_R_GB, _R_END)
        gbt_ref[0, :, j] = (gb * jax.nn.sigmoid(gb)).astype(bf16).T.reshape(DIFF_HEADS, LANES, tm)


def _pipelined_attention(k_ref, vt_ref, s_ref, load_q, finish, *, seq, n_q, tq, n_maps, key_chunk, pv_first):
    f32 = jnp.float32
    n_chunks = seq // key_chunk
    n_tiles = key_chunk // KEY_TILE
    assert n_chunks % 2 == 0
    n_steps = k_ref.shape[1] * n_q
    rows = vt_ref.shape[2]
    neg_inf = lambda shape: jnp.full(shape, -jnp.inf, f32)
    by_sublane = lambda x: x.reshape(KEY_TILE // 8, 8, tq)

    def score_tile(q, i, head, chunk, slot, t, cmax):
        lo = chunk * key_chunk + t * KEY_TILE
        s = jnp.dot(k_ref[0, head, lo:lo + KEY_TILE, :], q, preferred_element_type=f32)
        s_ref[i, slot, t * KEY_TILE:(t + 1) * KEY_TILE, :] = s
        return jnp.maximum(cmax, jnp.max(by_sublane(s), axis=0))

    def q_tile(step, cmax):
        head, qi = lax.div(step, n_q), lax.rem(step, n_q)
        step_next = jnp.minimum(step + 1, n_steps - 1)
        head_next = lax.div(step_next, n_q)
        qs = load_q(head, qi)
        qs_next = load_q(head_next, lax.rem(step_next, n_q))
        cmax = list(cmax)
        m = [neg_inf((1, tq)) for _ in range(n_maps)]
        acc = [jnp.zeros((rows, tq), f32) for _ in range(n_maps)]
        pending = [None] * n_maps
        pv = [None] * n_maps

        def issue_pv(i):
            p, lo, alpha_c, last = pending[i]
            d = jnp.dot(vt_ref[0, head, :, lo:lo + KEY_TILE], p, preferred_element_type=f32)
            pv[i] = d if pv[i] is None else pv[i] + d
            if last:
                acc[i] = alpha_c * acc[i] + pv[i]
                pv[i] = None
            pending[i] = None

        for c in range(n_chunks):
            m_new = [jnp.maximum(m[i], jnp.max(cmax[i], axis=0, keepdims=True)) for i in range(n_maps)]
            alpha = [jnp.exp2(m[i] - m_new[i]) for i in range(n_maps)]
            cmax_next = [neg_inf((8, tq)) for _ in range(n_maps)]
            for t in range(n_tiles):
                for i in range(n_maps):
                    if pv_first and pending[i] is not None:
                        issue_pv(i)
                    if c + 1 < n_chunks:
                        cmax_next[i] = score_tile(qs[i], i, head, c + 1, (c + 1) % 2, t, cmax_next[i])
                    else:
                        cmax_next[i] = score_tile(qs_next[i], i, head_next, 0, 0, t, cmax_next[i])
                    s = s_ref[i, c % 2, t * KEY_TILE:(t + 1) * KEY_TILE, :]
                    p = jnp.exp2(s - m_new[i]).astype(jnp.bfloat16)
                    if pending[i] is not None:
                        issue_pv(i)
                    pending[i] = (p, c * key_chunk + t * KEY_TILE, alpha[i], t == n_tiles - 1)
            m, cmax = m_new, cmax_next
        for i in range(n_maps):
            issue_pv(i)
        finish(head, qi, acc)
        return tuple(cmax)

    q0 = load_q(0, 0)
    cmax0 = [neg_inf((8, tq)) for _ in range(n_maps)]
    for t in range(n_tiles):
        for i in range(n_maps):
            cmax0[i] = score_tile(q0[i], i, 0, 0, 0, t, cmax0[i])
    lax.fori_loop(0, n_steps, q_tile, tuple(cmax0))


def _mla_attn_kernel(k_ref, qt_ref, vt_ref, gt_ref, o_ref, s_ref, *, seq, n_q):
    f32 = jnp.float32
    tq = qt_ref.shape[-1]
    per_step = MLA_TILES_PER_STEP

    def load_q(head, qi):
        return [qt_ref[0, head, per_step * qi + j] for j in range(per_step)]

    def finish(head, qi, accs):
        for j, acc in enumerate(accs):
            o = acc[:MLA_V] * (1.0 / acc[MLA_V:MLA_V + 1])
            tile = per_step * qi + j
            o_ref[0, head, tile] = (o * gt_ref[0, head, tile].astype(f32)).astype(o_ref.dtype)

    _pipelined_attention(k_ref, vt_ref, s_ref, load_q, finish, seq=seq, n_q=n_q // per_step, tq=tq,
                         n_maps=per_step, key_chunk=MLA_KEY_CHUNK, pv_first=True)


def _diff_attn_kernel(k_ref, qt_ref, vt_ref, gt_ref, lam_ref, gd_ref, o_ref, s_ref, *,
                      seq, n_q, lam_init):
    f32 = jnp.float32
    tq = qt_ref.shape[-1]
    lp = lam_ref[...].astype(f32)
    lam = (jnp.exp(jnp.sum(lp[0:1] * lp[1:2], axis=1, keepdims=True))
           - jnp.exp(jnp.sum(lp[2:3] * lp[3:4], axis=1, keepdims=True)) + lam_init)
    first = lax.broadcasted_iota(jnp.int32, (LANES, tq), 0) < DIFF_HEAD_DIM
    per_step = DIFF_TILES_PER_STEP

    def load_q(head, qi):
        maps = []
        for j in range(per_step):
            qt = qt_ref[0, head, per_step * qi + j]
            zero = jnp.zeros_like(qt)
            maps += [jnp.where(first, qt, zero), jnp.where(first, zero, qt)]
        return maps

    def finish(head, qi, accs):
        dv = 2 * DIFF_HEAD_DIM
        for j in range(per_step):
            a1, a2 = accs[2 * j], accs[2 * j + 1]
            o = a1[:dv] * (1.0 / a1[dv:dv + 1]) - lam * (a2[:dv] * (1.0 / a2[dv:dv + 1]))
            o = o * lax.rsqrt(jnp.mean(o * o, axis=0, keepdims=True) + DIFF_RMS_EPS)
            o = o * gd_ref[...] * (1.0 - lam_init)
            tile = per_step * qi + j
            o_ref[0, head, tile] = (o * gt_ref[0, head, tile].astype(f32)).astype(o_ref.dtype)

    _pipelined_attention(k_ref, vt_ref, s_ref, load_q, finish, seq=seq, n_q=n_q // per_step, tq=tq,
                         n_maps=2 * per_step, key_chunk=DIFF_KEY_CHUNK, pv_first=False)


def _out_kernel(x_ref, za_ref, zb_ref, wgm_ref, bm_ref, wa_ref, wb_ref, wo_ref, lg_ref, lb_ref, o_ref):
    bf16, f32 = jnp.bfloat16, jnp.float32
    tm = TOKEN_TILE

    def residual(j):
        rows = slice(j * tm, (j + 1) * tm)
        za = za_ref[0, :, j].astype(f32).reshape(MLA_WIDTH, tm).T.astype(bf16)
        zb = zb_ref[0, :, j].astype(f32).reshape(DIFF_WIDTH, tm).T.astype(bf16)
        ya = jnp.dot(za, wa_ref[...], preferred_element_type=f32)
        yb = jnp.dot(zb, wb_ref[...], preferred_element_type=f32)
        x = x_ref[0, rows, :]
        gm = jax.nn.sigmoid(jnp.dot(x.astype(bf16), wgm_ref[...], preferred_element_type=f32) + bm_ref[...])
        merged = gm[:, :D_MODEL] * ya + gm[:, D_MODEL:] * yb
        out = jnp.dot(merged.astype(bf16), wo_ref[...], preferred_element_type=f32)
        return DEEPNORM_ALPHA * x + out

    def layer_norm(j, r):
        mu = jnp.mean(r, axis=-1, keepdims=True)
        d = r - mu
        var = jnp.mean(d * d, axis=-1, keepdims=True)
        o_ref[0, j * tm:(j + 1) * tm, :] = d * lax.rsqrt(var + LN_EPS) * lg_ref[...] + lb_ref[...]

    r_prev = residual(0)
    for j in range(1, OUT_STEP_TILES):
        r = residual(j)
        layer_norm(j - 1, r_prev)
        r_prev = r
    layer_norm(OUT_STEP_TILES - 1, r_prev)


def _transpose_cast_kernel(wt_ref, o_ref):
    o_ref[...] = wt_ref[0].T.astype(o_ref.dtype)


def _layer_spec(shape):
    return lambda l: pl.BlockSpec((None,) + tuple(shape), lambda *_: (l,) + (0,) * len(shape))


def _prepare_weights(p):
    bf16 = jnp.bfloat16
    w_in_t = jnp.swapaxes(p["w_in"], 1, 2)

    def columns(first, count, name):
        return pl.pallas_call(
            _transpose_cast_kernel,
            grid=(DEPTH, count // WEIGHT_COLS),
            in_specs=[pl.BlockSpec((pl.Element(1), pl.Element(WEIGHT_COLS), pl.Element(D_MODEL)),
                                   lambda l, j: (l, pl.multiple_of(first + j * WEIGHT_COLS, 8), 0))],
            out_specs=pl.BlockSpec((None, D_MODEL, WEIGHT_COLS), lambda l, j: (l, 0, j)),
            out_shape=jax.ShapeDtypeStruct((DEPTH, D_MODEL, count), bf16),
            compiler_params=pltpu.CompilerParams(dimension_semantics=("arbitrary", "arbitrary"),
                                                 vmem_limit_bytes=VMEM_LIMIT_BYTES),
            name=name,
        )(w_in_t)

    w_first = columns(0, WEIGHT_COLS, "w_first")
    kr_cols = jnp.pad(w_first[:, :, _IN_KR:_IN_REST], ((0, 0), (0, 0), (MLA_NOPE, LANES - MLA_NOPE - MLA_ROPE)))
    w_lat = jnp.concatenate([w_first[:, :, :_IN_KR], kr_cols], axis=2)
    w_rest = columns(_IN_REST, _R_END, "w_rest")
    w_gm = columns(_IN_GM, 2 * D_MODEL, "w_gm")
    wq = p["w_q_up"].reshape(DEPTH, MLA_Q_LORA, MLA_HEADS, MLA_NOPE + MLA_ROPE)
    wq = jnp.pad(wq, ((0, 0), (0, 0), (0, 0), (0, LANES - MLA_NOPE - MLA_ROPE)))
    wq = wq.reshape(DEPTH, MLA_Q_LORA, MLA_HEADS * LANES).astype(bf16)
    wkv = p["w_kv_up"].reshape(DEPTH, MLA_KV_LORA, MLA_HEADS, MLA_NOPE + MLA_V)
    wkn = jnp.pad(wkv[..., :MLA_NOPE], ((0, 0), (0, 0), (0, 0), (0, LANES - MLA_NOPE)))
    wkn = wkn.reshape(DEPTH, MLA_KV_LORA, MLA_HEADS * LANES).astype(bf16)
    wv = wkv[..., MLA_NOPE:].reshape(DEPTH, MLA_KV_LORA, MLA_WIDTH).astype(bf16)
    return dict(
        w_lat=w_lat, w_rest=w_rest, w_gm=w_gm, wq=wq, wkn=wkn, wv=wv,
        g_q=p["g_q"][:, None], g_kv=p["g_kv"][:, None], b_merge=p["b_merge"][:, None],
        diff_lambda=p["diff_lambda"], g_diff=p["g_diff"][:, :, None],
        wa=p["w_branch_a"].astype(bf16), wb=p["w_branch_b"].astype(bf16), wo=p["w_out"].astype(bf16),
        ln_gamma=p["ln_gamma"][:, None], ln_beta=p["ln_beta"][:, None])


def _layer(x, w, l, tab_a, tab_b):
    bf16, f32 = jnp.bfloat16, jnp.float32
    batch, seq, _ = x.shape
    tm = TOKEN_TILE
    n_q = seq // tm
    step = STEP_TILES * tm
    cparams = functools.partial(pltpu.CompilerParams, vmem_limit_bytes=VMEM_LIMIT_BYTES)

    va_rows = MLA_V + ONES_ROWS
    vb_rows = 2 * DIFF_HEAD_DIM + ONES_ROWS
    tok = lambda b, i: (b, i, 0)
    tile5 = lambda b, i: (b, 0, i, 0, 0)
    lane4 = lambda b, i: (b, 0, 0, i)
    tab_spec = pl.BlockSpec((3, step, LANES), lambda b, i: (0, i, 0))
    ka, qat, vat, gat, kb, qbt, vbt, gbt = pl.pallas_call(
        _proj_kernel,
        grid=(batch, seq // step),
        in_specs=[
            pl.BlockSpec((1, step, D_MODEL), tok),
            _layer_spec(w["w_lat"].shape[1:])(l), _layer_spec(w["w_rest"].shape[1:])(l),
            _layer_spec(w["wq"].shape[1:])(l), _layer_spec(w["wkn"].shape[1:])(l),
            _layer_spec(w["wv"].shape[1:])(l),
            _layer_spec((1, MLA_Q_LORA))(l), _layer_spec((1, MLA_KV_LORA))(l),
            tab_spec, tab_spec,
        ],
        out_specs=[
            pl.BlockSpec((1, MLA_HEADS, step, LANES), lambda b, i: (b, 0, i, 0)),
            pl.BlockSpec((1, MLA_HEADS, STEP_TILES, LANES, tm), tile5),
            pl.BlockSpec((1, MLA_HEADS, va_rows, step), lane4),
            pl.BlockSpec((1, MLA_HEADS, STEP_TILES, MLA_V, tm), tile5),
            pl.BlockSpec((1, DIFF_HEADS, step, LANES), lambda b, i: (b, 0, i, 0)),
            pl.BlockSpec((1, DIFF_HEADS, STEP_TILES, LANES, tm), tile5),
            pl.BlockSpec((1, DIFF_HEADS, vb_rows, step), lane4),
            pl.BlockSpec((1, DIFF_HEADS, STEP_TILES, LANES, tm), tile5),
        ],
        out_shape=[
            jax.ShapeDtypeStruct((batch, MLA_HEADS, seq, LANES), bf16),
            jax.ShapeDtypeStruct((batch, MLA_HEADS, n_q, LANES, tm), bf16),
            jax.ShapeDtypeStruct((batch, MLA_HEADS, va_rows, seq), bf16),
            jax.ShapeDtypeStruct((batch, MLA_HEADS, n_q, MLA_V, tm), bf16),
            jax.ShapeDtypeStruct((batch, DIFF_HEADS, seq, LANES), bf16),
            jax.ShapeDtypeStruct((batch, DIFF_HEADS, n_q, LANES, tm), bf16),
            jax.ShapeDtypeStruct((batch, DIFF_HEADS, vb_rows, seq), bf16),
            jax.ShapeDtypeStruct((batch, DIFF_HEADS, n_q, LANES, tm), bf16),
        ],
        compiler_params=cparams(dimension_semantics=("arbitrary", "arbitrary")),
        name="proj",
    )(x, w["w_lat"], w["w_rest"], w["wq"], w["wkn"], w["wv"], w["g_q"], w["g_kv"], tab_a, tab_b)

    head4 = lambda b, h: (b, h, 0, 0)
    head5 = lambda b, h: (b, h, 0, 0, 0)
    za = pl.pallas_call(
        functools.partial(_mla_attn_kernel, seq=seq, n_q=n_q),
        grid=(batch, MLA_HEADS // MLA_HEADS_PER_STEP),
        in_specs=[
            pl.BlockSpec((1, MLA_HEADS_PER_STEP, seq, LANES), head4),
            pl.BlockSpec((1, MLA_HEADS_PER_STEP, n_q, LANES, tm), head5),
            pl.BlockSpec((1, MLA_HEADS_PER_STEP, va_rows, seq), head4),
            pl.BlockSpec((1, MLA_HEADS_PER_STEP, n_q, MLA_V, tm), head5),
        ],
        out_specs=pl.BlockSpec((1, MLA_HEADS_PER_STEP, n_q, MLA_V, tm), head5),
        out_shape=jax.ShapeDtypeStruct((batch, MLA_HEADS, n_q, MLA_V, tm), bf16),
        scratch_shapes=[pltpu.VMEM((MLA_TILES_PER_STEP, 2, MLA_KEY_CHUNK, tm), f32)],
        compiler_params=cparams(dimension_semantics=("arbitrary", "arbitrary")),
        name="mla_attn",
    )(ka, qat, vat, gat)

    lam_init = 0.8 - 0.6 * math.exp(-0.3 * l)
    zb = pl.pallas_call(
        functools.partial(_diff_attn_kernel, seq=seq, n_q=n_q, lam_init=lam_init),
        grid=(batch, DIFF_HEADS // DIFF_HEADS_PER_STEP),
        in_specs=[
            pl.BlockSpec((1, DIFF_HEADS_PER_STEP, seq, LANES), head4),
            pl.BlockSpec((1, DIFF_HEADS_PER_STEP, n_q, LANES, tm), head5),
            pl.BlockSpec((1, DIFF_HEADS_PER_STEP, vb_rows, seq), head4),
            pl.BlockSpec((1, DIFF_HEADS_PER_STEP, n_q, LANES, tm), head5),
            _layer_spec((4, DIFF_HEAD_DIM))(l),
            _layer_spec((2 * DIFF_HEAD_DIM, 1))(l),
        ],
        out_specs=pl.BlockSpec((1, DIFF_HEADS_PER_STEP, n_q, LANES, tm), head5),
        out_shape=jax.ShapeDtypeStruct((batch, DIFF_HEADS, n_q, LANES, tm), bf16),
        scratch_shapes=[pltpu.VMEM((2 * DIFF_TILES_PER_STEP, 2, DIFF_KEY_CHUNK, tm), f32)],
        compiler_params=cparams(dimension_semantics=("arbitrary", "arbitrary")),
        name="diff_attn",
    )(kb, qbt, vbt, gbt, w["diff_lambda"], w["g_diff"])

    out_step = OUT_STEP_TILES * tm
    return pl.pallas_call(
        _out_kernel,
        grid=(batch, seq // out_step),
        in_specs=[
            pl.BlockSpec((1, out_step, D_MODEL), tok),
            pl.BlockSpec((1, MLA_HEADS, OUT_STEP_TILES, MLA_V, tm), tile5),
            pl.BlockSpec((1, DIFF_HEADS, OUT_STEP_TILES, LANES, tm), tile5),
            _layer_spec((D_MODEL, 2 * D_MODEL))(l), _layer_spec((1, 2 * D_MODEL))(l),
            _layer_spec((MLA_WIDTH, D_MODEL))(l), _layer_spec((DIFF_WIDTH, D_MODEL))(l),
            _layer_spec((D_MODEL, D_MODEL))(l),
            _layer_spec((1, D_MODEL))(l), _layer_spec((1, D_MODEL))(l),
        ],
        out_specs=pl.BlockSpec((1, out_step, D_MODEL), tok),
        out_shape=jax.ShapeDtypeStruct((batch, seq, D_MODEL), f32),
        compiler_params=cparams(dimension_semantics=("arbitrary", "arbitrary")),
        name="out_proj",
    )(x, za, zb, w["w_gm"], w["b_merge"], w["wa"], w["wb"], w["wo"], w["ln_gamma"], w["ln_beta"])


def kernel(x, w_in, g_q, w_q_up, g_kv, w_kv_up, diff_lambda, g_diff, w_branch_a, w_branch_b,
           b_merge, w_out, ln_gamma, ln_beta):
    weights = _prepare_weights(dict(
        w_in=w_in, g_q=g_q, w_q_up=w_q_up, g_kv=g_kv, w_kv_up=w_kv_up, diff_lambda=diff_lambda,
        g_diff=g_diff, w_branch_a=w_branch_a, w_branch_b=w_branch_b, b_merge=b_merge, w_out=w_out,
        ln_gamma=ln_gamma, ln_beta=ln_beta))
    seq = x.shape[1]
    tab_a = _rotary_tables(seq, MLA_ROPE, MLA_NOPE, LANES)
    tab_b = _rotary_tables(seq, DIFF_ROT, 0, DIFF_HEAD_DIM)
    for l in range(DEPTH):
        x = _layer(x, weights, l, tab_a, tab_b)
    return x
```

```python
import functools
import math

import jax
import jax.numpy as jnp
from jax import lax
from jax.experimental import pallas as pl
from jax.experimental.pallas import tpu as pltpu

D_MODEL = 1024
DEPTH = 2
MLA_HEADS = 8
MLA_Q_LORA = 256
MLA_KV_LORA = 128
MLA_NOPE = 64
MLA_ROPE = 32
MLA_V = 64
MLA_WIDTH = MLA_HEADS * MLA_V
DIFF_HEADS = 4
DIFF_HEAD_DIM = 64
DIFF_WIDTH = DIFF_HEADS * 2 * DIFF_HEAD_DIM
DIFF_ROT = DIFF_HEAD_DIM // 4
ROPE_THETA = 500000.0
DEEPNORM_ALPHA = (2 * DEPTH) ** 0.25
LN_EPS = 1e-5
RMS_EPS = 1e-6
DIFF_RMS_EPS = 1e-5
LOG2_E = math.log2(math.e)

LANES = 128
TOKEN_TILE = 256
STEP_TILES = 4
OUT_STEP_TILES = 4
WEIGHT_COLS = 512
KEY_TILE = 256
MLA_KEY_CHUNK = 256
DIFF_KEY_CHUNK = 512
ONES_ROWS = 16
MLA_HEADS_PER_STEP = 4
DIFF_HEADS_PER_STEP = 2
MLA_TILES_PER_STEP = 4
DIFF_TILES_PER_STEP = 2
VMEM_LIMIT_BYTES = 56 * 1024 * 1024

_IN_KR = MLA_Q_LORA + MLA_KV_LORA
_IN_REST = _IN_KR + MLA_ROPE
_L_KV = MLA_Q_LORA
_L_KR = _L_KV + MLA_KV_LORA
_L_END = _L_KR + LANES
_R_GA = 0
_R_QD = _R_GA + MLA_WIDTH
_R_KD = _R_QD + DIFF_WIDTH
_R_VD = _R_KD + DIFF_WIDTH
_R_GB = _R_VD + DIFF_WIDTH
_R_END = _R_GB + DIFF_WIDTH
_IN_GM = _IN_REST + _R_END


def _rotary_tables(seq, rot_dim, first_lane, period):
    half = rot_dim // 2
    lane = jnp.arange(LANES) % period - first_lane
    lo = (lane >= 0) & (lane < half)
    hi = (lane >= half) & (lane < rot_dim)
    idx = jnp.clip(jnp.where(hi, lane - half, lane), 0, half - 1)
    inv_freq = ROPE_THETA ** (-idx.astype(jnp.float32) / half)
    ang = jnp.arange(seq, dtype=jnp.float32)[:, None] * inv_freq[None, :]
    cos, sin = jnp.cos(ang), jnp.sin(ang)
    c = jnp.where((lo | hi)[None, :], cos, 1.0)
    s_lo = jnp.where(lo[None, :], -sin, 0.0)
    s_hi = jnp.where(hi[None, :], sin, 0.0)
    return jnp.stack([c, s_lo, s_hi]).astype(jnp.float32)


def _rotate(x, tab, half):
    up = pltpu.roll(x, LANES - half, 1)
    down = pltpu.roll(x, half, 1)
    return x * tab[0] + up * tab[1] + down * tab[2]


def _rms_norm(x, g, eps):
    return x * lax.rsqrt(jnp.mean(x * x, axis=-1, keepdims=True) + eps) * g


def _with_ones_rows(vt):
    heads, _, tokens = vt.shape
    return jnp.concatenate([vt, jnp.ones((heads, ONES_ROWS, tokens), vt.dtype)], axis=1)


def _proj_kernel(x_ref, wlat_ref, wrest_ref, wq_ref, wkn_ref, wv_ref, gq_ref, gkv_ref,
                 taba_ref, tabb_ref,
                 ka_ref, qat_ref, vat_ref, gat_ref, kb_ref, qbt_ref, vbt_ref, gbt_ref):
    bf16, f32 = jnp.bfloat16, jnp.float32
    tm = TOKEN_TILE
    mla_scale = LOG2_E / math.sqrt(MLA_NOPE + MLA_ROPE)
    diff_scale = LOG2_E / math.sqrt(DIFF_HEAD_DIM)

    for j in range(STEP_TILES):
        rows = slice(j * tm, (j + 1) * tm)
        xb = x_ref[0, rows, :].astype(bf16)
        tab_a = [taba_ref[i, rows, :] for i in range(3)]
        tab_b = [tabb_ref[i, rows, :] for i in range(3)]

        def proj(lo, hi):
            return jnp.dot(xb, wrest_ref[:, lo:hi], preferred_element_type=f32)

        lat = jnp.dot(xb, wlat_ref[...], preferred_element_type=f32)
        cq = _rms_norm(lat[:, :_L_KV], gq_ref[...], RMS_EPS).astype(bf16)
        ckv = _rms_norm(lat[:, _L_KV:_L_KR], gkv_ref[...], RMS_EPS).astype(bf16)
        k_rope = _rotate(lat[:, _L_KR:_L_END], tab_a, MLA_ROPE // 2)
        q_all = jnp.dot(cq, wq_ref[...], preferred_element_type=f32)
        k_all = jnp.dot(ckv, wkn_ref[...], preferred_element_type=f32)
        for h in range(MLA_HEADS):
            sl = slice(h * LANES, (h + 1) * LANES)
            q_h = _rotate(q_all[:, sl], tab_a, MLA_ROPE // 2) * mla_scale
            qat_ref[0, h, j] = q_h.astype(bf16).T
            ka_ref[0, h, rows, :] = (k_all[:, sl] + k_rope).astype(bf16)
        v = jnp.dot(ckv, wv_ref[...], preferred_element_type=f32)
        vat_ref[0, :, :, rows] = _with_ones_rows(v.T.reshape(MLA_HEADS, MLA_V, tm)).astype(bf16)
        ga = proj(_R_GA, _R_QD)
        gat_ref[0, :, j] = (ga * jax.nn.sigmoid(ga)).astype(bf16).T.reshape(MLA_HEADS, MLA_V, tm)

        qk_d = proj(_R_QD, _R_VD)
        for h in range(DIFF_HEADS):
            sl = slice(h * LANES, (h + 1) * LANES)
            q_h = _rotate(qk_d[:, sl], tab_b, DIFF_ROT // 2)
            qbt_ref[0, h, j] = (q_h * diff_scale).astype(bf16).T
            k_h = _rotate(qk_d[:, DIFF_WIDTH + h * LANES:DIFF_WIDTH + (h + 1) * LANES], tab_b, DIFF_ROT // 2)
            kb_ref[0, h, rows, :] = k_h.astype(bf16)
        vbt_ref[0, :, :, rows] = _with_ones_rows(proj(_R_VD, _R_GB).T.reshape(DIFF_HEADS, LANES, tm)).astype(bf16)
        gb = proj(_R_GB, _R_END)
        gbt_ref[0, :, j] = (gb * jax.nn.sigmoid(gb)).astype(bf16).T.reshape(DIFF_HEADS, LANES, tm)


def _pipelined_attention(k_ref, vt_ref, s_ref, load_q, finish, *, seq, n_q, tq, n_maps, key_chunk, pv_first):
    f32 = jnp.float32
    n_chunks = seq // key_chunk
    n_tiles = key_chunk // KEY_TILE
    assert n_chunks % 2 == 0
    n_steps = k_ref.shape[1] * n_q
    rows = vt_ref.shape[2]
    neg_inf = lambda shape: jnp.full(shape, -jnp.inf, f32)
    by_sublane = lambda x: x.reshape(KEY_TILE // 8, 8, tq)

    def score_tile(q, i, head, chunk, slot, t, cmax):
        lo = chunk * key_chunk + t * KEY_TILE
        s = jnp.dot(k_ref[0, head, lo:lo + KEY_TILE, :], q, preferred_element_type=f32)
        s_ref[i, slot, t * KEY_TILE:(t + 1) * KEY_TILE, :] = s
        return jnp.maximum(cmax, jnp.max(by_sublane(s), axis=0))

    def q_tile(step, cmax):
        head, qi = lax.div(step, n_q), lax.rem(step, n_q)
        step_next = jnp.minimum(step + 1, n_steps - 1)
        head_next = lax.div(step_next, n_q)
        qs = load_q(head, qi)
        qs_next = load_q(head_next, lax.rem(step_next, n_q))
        cmax = list(cmax)
        m = [neg_inf((1, tq)) for _ in range(n_maps)]
        acc = [jnp.zeros((rows, tq), f32) for _ in range(n_maps)]
        pending = [None] * n_maps
        pv = [None] * n_maps

        def issue_pv(i):
            p, lo, alpha_c, last = pending[i]
            d = jnp.dot(vt_ref[0, head, :, lo:lo + KEY_TILE], p, preferred_element_type=f32)
            pv[i] = d if pv[i] is None else pv[i] + d
            if last:
                acc[i] = alpha_c * acc[i] + pv[i]
                pv[i] = None
            pending[i] = None

        for c in range(n_chunks):
            m_new = [jnp.maximum(m[i], jnp.max(cmax[i], axis=0, keepdims=True)) for i in range(n_maps)]
            alpha = [jnp.exp2(m[i] - m_new[i]) for i in range(n_maps)]
            cmax_next = [neg_inf((8, tq)) for _ in range(n_maps)]
            for t in range(n_tiles):
                for i in range(n_maps):
                    if pv_first and pending[i] is not None:
                        issue_pv(i)
                    if c + 1 < n_chunks:
                        cmax_next[i] = score_tile(qs[i], i, head, c + 1, (c + 1) % 2, t, cmax_next[i])
                    else:
                        cmax_next[i] = score_tile(qs_next[i], i, head_next, 0, 0, t, cmax_next[i])
                    s = s_ref[i, c % 2, t * KEY_TILE:(t + 1) * KEY_TILE, :]
                    p = jnp.exp2(s - m_new[i]).astype(jnp.bfloat16)
                    if pending[i] is not None:
                        issue_pv(i)
                    pending[i] = (p, c * key_chunk + t * KEY_TILE, alpha[i], t == n_tiles - 1)
            m, cmax = m_new, cmax_next
        for i in range(n_maps):
            issue_pv(i)
        finish(head, qi, acc)
        return tuple(cmax)

    q0 = load_q(0, 0)
    cmax0 = [neg_inf((8, tq)) for _ in range(n_maps)]
    for t in range(n_tiles):
        for i in range(n_maps):
            cmax0[i] = score_tile(q0[i], i, 0, 0, 0, t, cmax0[i])
    lax.fori_loop(0, n_steps, q_tile, tuple(cmax0))


def _mla_attn_kernel(k_ref, qt_ref, vt_ref, gt_ref, o_ref, s_ref, *, seq, n_q):
    f32 = jnp.float32
    tq = qt_ref.shape[-1]
    per_step = MLA_TILES_PER_STEP

    def load_q(head, qi):
        return [qt_ref[0, head, per_step * qi + j] for j in range(per_step)]

    def finish(head, qi, accs):
        for j, acc in enumerate(accs):
            o = acc[:MLA_V] * (1.0 / acc[MLA_V:MLA_V + 1])
            tile = per_step * qi + j
            o_ref[0, head, tile] = (o * gt_ref[0, head, tile].astype(f32)).astype(o_ref.dtype)

    _pipelined_attention(k_ref, vt_ref, s_ref, load_q, finish, seq=seq, n_q=n_q // per_step, tq=tq,
                         n_maps=per_step, key_chunk=MLA_KEY_CHUNK, pv_first=True)


def _diff_attn_kernel(k_ref, qt_ref, vt_ref, gt_ref, lam_ref, gd_ref, o_ref, s_ref, *,
                      seq, n_q, lam_init):
    f32 = jnp.float32
    tq = qt_ref.shape[-1]
    lp = lam_ref[...].astype(f32)
    lam = (jnp.exp(jnp.sum(lp[0:1] * lp[1:2], axis=1, keepdims=True))
           - jnp.exp(jnp.sum(lp[2:3] * lp[3:4], axis=1, keepdims=True)) + lam_init)
    first = lax.broadcasted_iota(jnp.int32, (LANES, tq), 0) < DIFF_HEAD_DIM
    per_step = DIFF_TILES_PER_STEP

    def load_q(head, qi):
        maps = []
        for j in range(per_step):
            qt = qt_ref[0, head, per_step * qi + j]
            zero = jnp.zeros_like(qt)
            maps += [jnp.where(first, qt, zero), jnp.where(first, zero, qt)]
        return maps

    def finish(head, qi, accs):
        dv = 2 * DIFF_HEAD_DIM
        for j in range(per_step):
            a1, a2 = accs[2 * j], accs[2 * j + 1]
            o = a1[:dv] * (1.0 / a1[dv:dv + 1]) - lam * (a2[:dv] * (1.0 / a2[dv:dv + 1]))
            o = o * lax.rsqrt(jnp.mean(o * o, axis=0, keepdims=True) + DIFF_RMS_EPS)
            o = o * gd_ref[...] * (1.0 - lam_init)
            tile = per_step * qi + j
            o_ref[0, head, tile] = (o * gt_ref[0, head, tile].astype(f32)).astype(o_ref.dtype)

    _pipelined_attention(k_ref, vt_ref, s_ref, load_q, finish, seq=seq, n_q=n_q // per_step, tq=tq,
                         n_maps=2 * per_step, key_chunk=DIFF_KEY_CHUNK, pv_first=False)


def _out_kernel(x_ref, za_ref, zb_ref, wgm_ref, bm_ref, wa_ref, wb_ref, wo_ref, lg_ref, lb_ref, o_ref):
    bf16, f32 = jnp.bfloat16, jnp.float32
    tm = TOKEN_TILE

    def residual(j):
        rows = slice(j * tm, (j + 1) * tm)
        za = za_ref[0, :, j].astype(f32).reshape(MLA_WIDTH, tm).T.astype(bf16)
        zb = zb_ref[0, :, j].astype(f32).reshape(DIFF_WIDTH, tm).T.astype(bf16)
        ya = jnp.dot(za, wa_ref[...], preferred_element_type=f32)
        yb = jnp.dot(zb, wb_ref[...], preferred_element_type=f32)
        x = x_ref[0, rows, :]
        gm = jax.nn.sigmoid(jnp.dot(x.astype(bf16), wgm_ref[...], preferred_element_type=f32) + bm_ref[...])
        merged = gm[:, :D_MODEL] * ya + gm[:, D_MODEL:] * yb
        out = jnp.dot(merged.astype(bf16), wo_ref[...], preferred_element_type=f32)
        return DEEPNORM_ALPHA * x + out

    def layer_norm(j, r):
        mu = jnp.mean(r, axis=-1, keepdims=True)
        d = r - mu
        var = jnp.mean(d * d, axis=-1, keepdims=True)
        o_ref[0, j * tm:(j + 1) * tm, :] = d * lax.rsqrt(var + LN_EPS) * lg_ref[...] + lb_ref[...]

    r_prev = residual(0)
    for j in range(1, OUT_STEP_TILES):
        r = residual(j)
        layer_norm(j - 1, r_prev)
        r_prev = r
    layer_norm(OUT_STEP_TILES - 1, r_prev)


def _transpose_cast_kernel(wt_ref, o_ref):
    o_ref[...] = wt_ref[0].T.astype(o_ref.dtype)


def _layer_spec(shape):
    return lambda l: pl.BlockSpec((None,) + tuple(shape), lambda *_: (l,) + (0,) * len(shape))


def _prepare_weights(p):
    bf16 = jnp.bfloat16
    w_in_t = jnp.swapaxes(p["w_in"], 1, 2)

    def columns(first, count, name):
        block = count // 2 if count > WEIGHT_COLS else count
        return pl.pallas_call(
            _transpose_cast_kernel,
            grid=(DEPTH, count // block),
            in_specs=[pl.BlockSpec((pl.Element(1), pl.Element(block), pl.Element(D_MODEL)),
                                   lambda l, j: (l, pl.multiple_of(first + j * block, 8), 0))],
            out_specs=pl.BlockSpec((None, D_MODEL, block), lambda l, j: (l, 0, j)),
            out_shape=jax.ShapeDtypeStruct((DEPTH, D_MODEL, count), bf16),
            compiler_params=pltpu.CompilerParams(dimension_semantics=("arbitrary", "arbitrary"),
                                                 vmem_limit_bytes=VMEM_LIMIT_BYTES),
            name=name,
        )(w_in_t)

    w_first = columns(0, WEIGHT_COLS, "w_first")
    kr_cols = jnp.pad(w_first[:, :, _IN_KR:_IN_REST], ((0, 0), (0, 0), (MLA_NOPE, LANES - MLA_NOPE - MLA_ROPE)))
    w_lat = jnp.concatenate([w_first[:, :, :_IN_KR], kr_cols], axis=2)
    w_rest = columns(_IN_REST, _R_END, "w_rest")
    w_gm = columns(_IN_GM, 2 * D_MODEL, "w_gm")
    wq = p["w_q_up"].reshape(DEPTH, MLA_Q_LORA, MLA_HEADS, MLA_NOPE + MLA_ROPE)
    wq = jnp.pad(wq, ((0, 0), (0, 0), (0, 0), (0, LANES - MLA_NOPE - MLA_ROPE)))
    wq = wq.reshape(DEPTH, MLA_Q_LORA, MLA_HEADS * LANES).astype(bf16)
    wkv = p["w_kv_up"].reshape(DEPTH, MLA_KV_LORA, MLA_HEADS, MLA_NOPE + MLA_V)
    wkn = jnp.pad(wkv[..., :MLA_NOPE], ((0, 0), (0, 0), (0, 0), (0, LANES - MLA_NOPE)))
    wkn = wkn.reshape(DEPTH, MLA_KV_LORA, MLA_HEADS * LANES).astype(bf16)
    wv = wkv[..., MLA_NOPE:].reshape(DEPTH, MLA_KV_LORA, MLA_WIDTH).astype(bf16)
    return dict(
        w_lat=w_lat, w_rest=w_rest, w_gm=w_gm, wq=wq, wkn=wkn, wv=wv,
        g_q=p["g_q"][:, None], g_kv=p["g_kv"][:, None], b_merge=p["b_merge"][:, None],
        diff_lambda=p["diff_lambda"], g_diff=p["g_diff"][:, :, None],
        wa=p["w_branch_a"].astype(bf16), wb=p["w_branch_b"].astype(bf16), wo=p["w_out"].astype(bf16),
        ln_gamma=p["ln_gamma"][:, None], ln_beta=p["ln_beta"][:, None])


def _layer(x, w, l, tab_a, tab_b):
    bf16, f32 = jnp.bfloat16, jnp.float32
    batch, seq, _ = x.shape
    tm = TOKEN_TILE
    n_q = seq // tm
    step = STEP_TILES * tm
    cparams = functools.partial(pltpu.CompilerParams, vmem_limit_bytes=VMEM_LIMIT_BYTES)

    va_rows = MLA_V + ONES_ROWS
    vb_rows = 2 * DIFF_HEAD_DIM + ONES_ROWS
    tok = lambda b, i: (b, i, 0)
    tile5 = lambda b, i: (b, 0, i, 0, 0)
    lane4 = lambda b, i: (b, 0, 0, i)
    tab_spec = pl.BlockSpec((3, step, LANES), lambda b, i: (0, i, 0))
    ka, qat, vat, gat, kb, qbt, vbt, gbt = pl.pallas_call(
        _proj_kernel,
        grid=(batch, seq // step),
        in_specs=[
            pl.BlockSpec((1, step, D_MODEL), tok),
            _layer_spec(w["w_lat"].shape[1:])(l), _layer_spec(w["w_rest"].shape[1:])(l),
            _layer_spec(w["wq"].shape[1:])(l), _layer_spec(w["wkn"].shape[1:])(l),
            _layer_spec(w["wv"].shape[1:])(l),
            _layer_spec((1, MLA_Q_LORA))(l), _layer_spec((1, MLA_KV_LORA))(l),
            tab_spec, tab_spec,
        ],
        out_specs=[
            pl.BlockSpec((1, MLA_HEADS, step, LANES), lambda b, i: (b, 0, i, 0)),
            pl.BlockSpec((1, MLA_HEADS, STEP_TILES, LANES, tm), tile5),
            pl.BlockSpec((1, MLA_HEADS, va_rows, step), lane4),
            pl.BlockSpec((1, MLA_HEADS, STEP_TILES, MLA_V, tm), tile5),
            pl.BlockSpec((1, DIFF_HEADS, step, LANES), lambda b, i: (b, 0, i, 0)),
            pl.BlockSpec((1, DIFF_HEADS, STEP_TILES, LANES, tm), tile5),
            pl.BlockSpec((1, DIFF_HEADS, vb_rows, step), lane4),
            pl.BlockSpec((1, DIFF_HEADS, STEP_TILES, LANES, tm), tile5),
        ],
        out_shape=[
            jax.ShapeDtypeStruct((batch, MLA_HEADS, seq, LANES), bf16),
            jax.ShapeDtypeStruct((batch, MLA_HEADS, n_q, LANES, tm), bf16),
            jax.ShapeDtypeStruct((batch, MLA_HEADS, va_rows, seq), bf16),
            jax.ShapeDtypeStruct((batch, MLA_HEADS, n_q, MLA_V, tm), bf16),
            jax.ShapeDtypeStruct((batch, DIFF_HEADS, seq, LANES), bf16),
            jax.ShapeDtypeStruct((batch, DIFF_HEADS, n_q, LANES, tm), bf16),
            jax.ShapeDtypeStruct((batch, DIFF_HEADS, vb_rows, seq), bf16),
            jax.ShapeDtypeStruct((batch, DIFF_HEADS, n_q, LANES, tm), bf16),
        ],
        compiler_params=cparams(dimension_semantics=("arbitrary", "arbitrary")),
        name="proj",
    )(x, w["w_lat"], w["w_rest"], w["wq"], w["wkn"], w["wv"], w["g_q"], w["g_kv"], tab_a, tab_b)

    head4 = lambda b, h: (b, h, 0, 0)
    head5 = lambda b, h: (b, h, 0, 0, 0)
    za = pl.pallas_call(
        functools.partial(_mla_attn_kernel, seq=seq, n_q=n_q),
        grid=(batch, MLA_HEADS // MLA_HEADS_PER_STEP),
        in_specs=[
            pl.BlockSpec((1, MLA_HEADS_PER_STEP, seq, LANES), head4),
            pl.BlockSpec((1, MLA_HEADS_PER_STEP, n_q, LANES, tm), head5),
            pl.BlockSpec((1, MLA_HEADS_PER_STEP, va_rows, seq), head4),
            pl.BlockSpec((1, MLA_HEADS_PER_STEP, n_q, MLA_V, tm), head5),
        ],
        out_specs=pl.BlockSpec((1, MLA_HEADS_PER_STEP, n_q, MLA_V, tm), head5),
        out_shape=jax.ShapeDtypeStruct((batch, MLA_HEADS, n_q, MLA_V, tm), bf16),
        scratch_shapes=[pltpu.VMEM((MLA_TILES_PER_STEP, 2, MLA_KEY_CHUNK, tm), f32)],
        compiler_params=cparams(dimension_semantics=("arbitrary", "arbitrary")),
        name="mla_attn",
    )(ka, qat, vat, gat)

    lam_init = 0.8 - 0.6 * math.exp(-0.3 * l)
    zb = pl.pallas_call(
        functools.partial(_diff_attn_kernel, seq=seq, n_q=n_q, lam_init=lam_init),
        grid=(batch, DIFF_HEADS // DIFF_HEADS_PER_STEP),
        in_specs=[
            pl.BlockSpec((1, DIFF_HEADS_PER_STEP, seq, LANES), head4),
            pl.BlockSpec((1, DIFF_HEADS_PER_STEP, n_q, LANES, tm), head5),
            pl.BlockSpec((1, DIFF_HEADS_PER_STEP, vb_rows, seq), head4),
            pl.BlockSpec((1, DIFF_HEADS_PER_STEP, n_q, LANES, tm), head5),
            _layer_spec((4, DIFF_HEAD_DIM))(l),
            _layer_spec((2 * DIFF_HEAD_DIM, 1))(l),
        ],
        out_specs=pl.BlockSpec((1, DIFF_HEADS_PER_STEP, n_q, LANES, tm), head5),
        out_shape=jax.ShapeDtypeStruct((batch, DIFF_HEADS, n_q, LANES, tm), bf16),
        scratch_shapes=[pltpu.VMEM((2 * DIFF_TILES_PER_STEP, 2, DIFF_KEY_CHUNK, tm), f32)],
        compiler_params=cparams(dimension_semantics=("arbitrary", "arbitrary")),
        name="diff_attn",
    )(kb, qbt, vbt, gbt, w["diff_lambda"], w["g_diff"])

    out_step = OUT_STEP_TILES * tm
    return pl.pallas_call(
        _out_kernel,
        grid=(batch, seq // out_step),
        in_specs=[
            pl.BlockSpec((1, out_step, D_MODEL), tok),
            pl.BlockSpec((1, MLA_HEADS, OUT_STEP_TILES, MLA_V, tm), tile5),
            pl.BlockSpec((1, DIFF_HEADS, OUT_STEP_TILES, LANES, tm), tile5),
            _layer_spec((D_MODEL, 2 * D_MODEL))(l), _layer_spec((1, 2 * D_MODEL))(l),
            _layer_spec((MLA_WIDTH, D_MODEL))(l), _layer_spec((DIFF_WIDTH, D_MODEL))(l),
            _layer_spec((D_MODEL, D_MODEL))(l),
            _layer_spec((1, D_MODEL))(l), _layer_spec((1, D_MODEL))(l),
        ],
        out_specs=pl.BlockSpec((1, out_step, D_MODEL), tok),
        out_shape=jax.ShapeDtypeStruct((batch, seq, D_MODEL), f32),
        compiler_params=cparams(dimension_semantics=("arbitrary", "arbitrary")),
        name="out_proj",
    )(x, za, zb, w["w_gm"], w["b_merge"], w["wa"], w["wb"], w["wo"], w["ln_gamma"], w["ln_beta"])


def kernel(x, w_in, g_q, w_q_up, g_kv, w_kv_up, diff_lambda, g_diff, w_branch_a, w_branch_b,
           b_merge, w_out, ln_gamma, ln_beta):
    weights = _prepare_weights(dict(
        w_in=w_in, g_q=g_q, w_q_up=w_q_up, g_kv=g_kv, w_kv_up=w_kv_up, diff_lambda=diff_lambda,
        g_diff=g_diff, w_branch_a=w_branch_a, w_branch_b=w_branch_b, b_merge=b_merge, w_out=w_out,
        ln_gamma=ln_gamma, ln_beta=ln_beta))
    seq = x.shape[1]
    tab_a = _rotary_tables(seq, MLA_ROPE, MLA_NOPE, LANES)
    tab_b = _rotary_tables(seq, DIFF_ROT, 0, DIFF_HEAD_DIM)
    for l in range(DEPTH):
        x = _layer(x, weights, l, tab_a, tab_b)
    return x
```
